```python
import math
import jax, jax.numpy as jnp
from jax import lax
import numpy as np

D_MODEL = 1024
BATCH = 2
SEQ = 8192
DEPTH = 4

N_MIXERS = 4
D_FF = 4 * D_MODEL
RMS_EPS = 1e-6
CONV_KERNEL = 31
SHORT_CONV = 3
FOX_HEADS = 16
FOX_HEAD_DIM = D_MODEL // FOX_HEADS
Q_BLOCK = 128
GLA_HEADS = 4
GLA_DK = D_MODEL // 2
GLA_DV = D_MODEL
GLA_DK_HEAD = GLA_DK // GLA_HEADS
GLA_DV_HEAD = GLA_DV // GLA_HEADS
GLA_GATE_RANK = 16
GLA_TAU = 16.0
GLA_CHUNK = 64

kernel_name = "interleaved_conv_fox_gla_hybrid"


def _layers_of(m):
    return len(range(m, DEPTH, N_MIXERS))


def rms_norm(x, g):
    x32 = x.astype(jnp.float32)
    y = x32 * lax.rsqrt(jnp.mean(x32 * x32, axis=-1, keepdims=True) + RMS_EPS)
    return (y * g.astype(jnp.float32)).astype(x.dtype)


def layer_norm(x, g, b):
    x32 = x.astype(jnp.float32)
    mu = jnp.mean(x32, axis=-1, keepdims=True)
    xc = x32 - mu
    y = xc * lax.rsqrt(jnp.mean(xc * xc, axis=-1, keepdims=True) + RMS_EPS)
    return (y * g.astype(jnp.float32) + b.astype(jnp.float32)).astype(x.dtype)


def conformer_conv(x, w_in, b_in, conv_w, conv_b, ln_g, ln_b, w_out, b_out):
    a, g = jnp.split(x @ w_in + b_in, 2, axis=-1)
    h = a * jax.nn.sigmoid(g)
    h = lax.conv_general_dilated(
        h, conv_w[:, None, :].astype(h.dtype), window_strides=(1,),
        padding=[(CONV_KERNEL - 1, 0)],
        dimension_numbers=("NWC", "WIO", "NWC"),
        feature_group_count=D_MODEL) + conv_b
    h = jax.nn.silu(layer_norm(h, ln_g, ln_b))
    return h @ w_out + b_out


def short_gated_conv(x, w_in, conv_w, w_out):
    S = x.shape[1]
    gate_b, gate_c, h = jnp.split(x @ w_in, 3, axis=-1)
    u = gate_c * h
    up = jnp.pad(u, ((0, 0), (SHORT_CONV - 1, 0), (0, 0)))
    conv = conv_w[0] * up[:, 0:S]
    for k in range(1, SHORT_CONV):
        conv = conv + conv_w[k] * up[:, k:k + S]
    return (gate_b * conv) @ w_out


def forgetting_attention(x, w_qkv, q_g, k_g, w_f, b_f, w_out):
    Bsz, S, _ = x.shape
    q, k, v = jnp.split(x @ w_qkv, 3, axis=-1)
    heads = lambda t: t.reshape(Bsz, S, FOX_HEADS, FOX_HEAD_DIM).transpose(0, 2, 1, 3)
    q = rms_norm(heads(q), q_g)
    k = rms_norm(heads(k), k_g)
    v = heads(v)
    log_f = jax.nn.log_sigmoid((x @ w_f + b_f).astype(jnp.float32))
    c = jnp.cumsum(log_f, axis=1).transpose(0, 2, 1)
    n_blk = S // Q_BLOCK
    qb = q.reshape(Bsz, FOX_HEADS, n_blk, Q_BLOCK, FOX_HEAD_DIM).transpose(2, 0, 1, 3, 4)
    cb = c.reshape(Bsz, FOX_HEADS, n_blk, Q_BLOCK).transpose(2, 0, 1, 3)
    starts = jnp.arange(n_blk, dtype=jnp.int32) * Q_BLOCK
    key_pos = jnp.arange(S, dtype=jnp.int32)
    scale = FOX_HEAD_DIM ** -0.5

    def block(args):
        q_i, c_i, start = args
        s = jnp.einsum("bhqd,bhkd->bhqk", q_i, k).astype(jnp.float32) * scale
        s = s + c_i[..., :, None] - c[:, :, None, :]
        q_pos = start + jnp.arange(Q_BLOCK, dtype=jnp.int32)
        s = jnp.where(key_pos[None, :] <= q_pos[:, None], s, -jnp.inf)
        p = jax.nn.softmax(s, axis=-1)
        return jnp.einsum("bhqk,bhkd->bhqd", p.astype(v.dtype), v)

    o = lax.map(block, (qb, cb, starts))
    o = o.transpose(1, 0, 3, 2, 4).reshape(Bsz, S, D_MODEL)
    return o @ w_out


def gated_linear_attention(x, w_in, w_g1, w_g2, b_g, o_g, w_out):
    Bsz, S, _ = x.shape
    q, k, v, r = jnp.split(x @ w_in, [GLA_DK, 2 * GLA_DK, 2 * GLA_DK + GLA_DV], axis=-1)
    log_a = jax.nn.log_sigmoid(((x @ w_g1) @ w_g2 + b_g).astype(jnp.float32)) / GLA_TAU
    nc = S // GLA_CHUNK

    def chunks(t, dh):
        return t.astype(jnp.float32).reshape(Bsz, nc, GLA_CHUNK, GLA_HEADS, dh).transpose(1, 0, 3, 2, 4)

    qc = chunks(q, GLA_DK_HEAD) * (GLA_DK_HEAD ** -0.5)
    kc = chunks(k, GLA_DK_HEAD)
    vc = chunks(v, GLA_DV_HEAD)
    b = jnp.cumsum(chunks(log_a, GLA_DK_HEAD), axis=-2)
    b_last = b[..., -1:, :]
    q_t = qc * jnp.exp(b)
    k_t = kc * jnp.exp(-b)
    k_dec = kc * jnp.exp(b_last - b)
    causal = jnp.tril(jnp.ones((GLA_CHUNK, GLA_CHUNK), dtype=bool))
    att = jnp.where(causal, jnp.einsum("nbhtd,nbhsd->nbhts", q_t, k_t), 0.0)
    o_intra = jnp.einsum("nbhts,nbhse->nbhte", att, vc)

    def step(state, inp):
        q_i, kd_i, v_i, dl_i = inp
        o_i = jnp.einsum("bhtd,bhde->bhte", q_i, state)
        state = state * jnp.exp(dl_i)[..., None] + jnp.einsum("bhsd,bhse->bhde", kd_i, v_i)
        return state, o_i

    s0 = jnp.zeros((Bsz, GLA_HEADS, GLA_DK_HEAD, GLA_DV_HEAD), jnp.float32)
    _, o_inter = lax.scan(step, s0, (q_t, k_dec, vc, b_last[..., 0, :]))
    o = rms_norm(o_intra + o_inter, o_g)
    o = o.transpose(1, 0, 3, 2, 4).reshape(Bsz, S, GLA_DV).astype(x.dtype)
    return (o * jax.nn.silu(r)) @ w_out


def squared_relu_mlp(x, w1, w2):
    return jnp.square(jax.nn.relu(x @ w1)) @ w2


def setup_inputs(seed: int = 0) -> dict:
    key = jax.random.key(seed)
    ks = iter(jax.random.split(key, 40))
    f32 = jnp.float32

    def nrm(shape, scale):
        return jax.random.normal(next(ks), shape, f32) * scale

    def gain(shape):
        return 1.0 + nrm(shape, 0.05)

    nA, nB, nC, nD = (_layers_of(m) for m in range(N_MIXERS))
    D = D_MODEL
    return {
        "x": nrm((BATCH, SEQ, D), 1.0),
        "mix_norm": gain((DEPTH, D)),
        "mlp_norm": gain((DEPTH, D)),
        "mlp_w1": nrm((DEPTH, D, D_FF), D ** -0.5),
        "mlp_w2": nrm((DEPTH, D_FF, D), D_FF ** -0.5),
        "a_w_in": nrm((nA, D, 2 * D), D ** -0.5),
        "a_b_in": nrm((nA, 2 * D), 0.02),
        "a_conv_w": nrm((nA, CONV_KERNEL, D), CONV_KERNEL ** -0.5),
        "a_conv_b": nrm((nA, D), 0.02),
        "a_ln_g": gain((nA, D)),
        "a_ln_b": nrm((nA, D), 0.02),
        "a_w_out": nrm((nA, D, D), D ** -0.5),
        "a_b_out": nrm((nA, D), 0.02),
        "b_w_in": nrm((nB, D, 3 * D), D ** -0.5),
        "b_conv_w": nrm((nB, SHORT_CONV, D), SHORT_CONV ** -0.5),
        "b_w_out": nrm((nB, D, D), D ** -0.5),
        "c_w_qkv": nrm((nC, D, 3 * D), D ** -0.5),
        "c_q_norm": gain((nC, FOX_HEAD_DIM)),
        "c_k_norm": gain((nC, FOX_HEAD_DIM)),
        "c_w_f": nrm((nC, D, FOX_HEADS), 0.1 * D ** -0.5),
        "c_b_f": 2.0 + nrm((nC, FOX_HEADS), 1.0),
        "c_w_out": nrm((nC, D, D), D ** -0.5),
        "d_w_in": nrm((nD, D, 2 * GLA_DK + 2 * GLA_DV), D ** -0.5),
        "d_w_g1": nrm((nD, D, GLA_GATE_RANK), D ** -0.5),
        "d_w_g2": nrm((nD, GLA_GATE_RANK, GLA_DK), GLA_GATE_RANK ** -0.5),
        "d_b_g": nrm((nD, GLA_DK), 0.02),
        "d_o_norm": gain((nD, GLA_DV_HEAD)),
        "d_w_out": nrm((nD, GLA_DV, D), GLA_DV ** -0.5),
    }


def reference(x, mix_norm, mlp_norm, mlp_w1, mlp_w2,
              a_w_in, a_b_in, a_conv_w, a_conv_b, a_ln_g, a_ln_b, a_w_out, a_b_out,
              b_w_in, b_conv_w, b_w_out,
              c_w_qkv, c_q_norm, c_k_norm, c_w_f, c_b_f, c_w_out,
              d_w_in, d_w_g1, d_w_g2, d_b_g, d_o_norm, d_w_out):
    for i in range(DEPTH):
        m, j = i % N_MIXERS, i // N_MIXERS
        h = rms_norm(x, mix_norm[i])
        if m == 0:
            y = conformer_conv(h, a_w_in[j], a_b_in[j], a_conv_w[j], a_conv_b[j],
                               a_ln_g[j], a_ln_b[j], a_w_out[j], a_b_out[j])
        elif m == 1:
            y = short_gated_conv(h, b_w_in[j], b_conv_w[j], b_w_out[j])
        elif m == 2:
            y = forgetting_attention(h, c_w_qkv[j], c_q_norm[j], c_k_norm[j],
                                     c_w_f[j], c_b_f[j], c_w_out[j])
        else:
            y = gated_linear_attention(h, d_w_in[j], d_w_g1[j], d_w_g2[j], d_b_g[j],
                                       d_o_norm[j], d_w_out[j])
        x = x + y
        x = x + squared_relu_mlp(rms_norm(x, mlp_norm[i]), mlp_w1[i], mlp_w2[i])
    return x
```

```python
import functools

import numpy as np
import jax
import jax.numpy as jnp
from jax import lax
from jax.experimental import pallas as pl
from jax.experimental.pallas import tpu as pltpu

F32 = jnp.float32
BF16 = jnp.bfloat16

D_MODEL = 1024
D_FF = 4 * D_MODEL
RMS_EPS = 1e-6
CONV_KERNEL = 31
SHORT_CONV = 3
FOX_HEADS = 16
FOX_HEAD_DIM = D_MODEL // FOX_HEADS
GLA_HEADS = 4
GLA_DK = D_MODEL // 2
GLA_DV = D_MODEL
GLA_DK_HEAD = GLA_DK // GLA_HEADS
GLA_DV_HEAD = GLA_DV // GLA_HEADS
GLA_GATE_RANK = 16
GLA_TAU = 16.0
GLA_CHUNK = 64

LANES = 128
CONV_HALO = 32
SHORT_HALO = 8
NEG_BIG = -1e30
VMEM_LIMIT = 56 * 1024 * 1024

_NT = (((1,), (1,)), ((), ()))
_TN = (((0,), (0,)), ((), ()))


def _dot(a, b):
    return jnp.dot(a, b, preferred_element_type=F32)


def _rms(x, g):
    return x * lax.rsqrt(jnp.mean(x * x, axis=-1, keepdims=True) + RMS_EPS) * g


def _sigmoid(x):
    return 1.0 / (1.0 + jnp.exp(-x))


def _log_sigmoid(x):
    return jnp.minimum(x, 0.0) - jnp.log1p(jnp.exp(-jnp.abs(x)))


def _split3(x):
    h1 = x.astype(BF16)
    r1 = x - h1.astype(F32)
    h2 = r1.astype(BF16)
    h3 = (r1 - h2.astype(F32)).astype(BF16)
    return h1, h2, h3


def _const_spec(shape):
    nd = len(shape)
    return pl.BlockSpec(shape, lambda *_: (0,) * nd, pipeline_mode=pl.Buffered(1))


def _params(*sem):
    return pltpu.CompilerParams(dimension_semantics=sem, vmem_limit_bytes=VMEM_LIMIT)


MLP_TM = 512
MLP_TF = 1024


def _mlp_kernel(x_ref, y_ref, wo_ref, bo_ref, g_ref, w1_ref, w2_ref, o_ref):
    x1 = x_ref[...] + _dot(y_ref[...], wo_ref[...]) + bo_ref[...]
    hn = _rms(x1, g_ref[...]).astype(BF16)
    acc = x1
    for c in range(D_FF // MLP_TF):
        cols = slice(c * MLP_TF, (c + 1) * MLP_TF)
        h = jnp.maximum(_dot(hn, w1_ref[:, cols]), 0.0)
        acc = acc + _dot((h * h).astype(BF16), w2_ref[cols, :])
    o_ref[...] = acc


def _proj_mlp(x2d, y2d, wo, bo, g, w1, w2):
    T = x2d.shape[0]
    tm = min(MLP_TM, T)
    row = lambda i: (i, 0)
    return pl.pallas_call(
        _mlp_kernel,
        out_shape=jax.ShapeDtypeStruct((T, D_MODEL), F32),
        grid=(T // tm,),
        in_specs=[
            pl.BlockSpec((tm, D_MODEL), row),
            pl.BlockSpec((tm, D_MODEL), row),
            _const_spec((D_MODEL, D_MODEL)),
            _const_spec((1, D_MODEL)),
            _const_spec((1, D_MODEL)),
            _const_spec((D_MODEL, D_FF)),
            _const_spec((D_FF, D_MODEL)),
        ],
        out_specs=pl.BlockSpec((tm, D_MODEL), row),
        compiler_params=_params("parallel"),
        name="proj_mlp",
    )(x2d, y2d, wo, bo, g, w1, w2)


SEQ_TILE = 512


def _conformer_kernel(xh_ref, x_ref, g_ref, win_ref, bin_ref, cw_ref, cb_ref, lg_ref, lb_ref,
                      y_ref, glu_ref, *, ts):
    i = pl.program_id(1)
    xa = jnp.concatenate([xh_ref[...], x_ref[...]], axis=0)
    hn = _rms(xa, g_ref[...]).astype(BF16)
    p = _dot(hn, win_ref[...]) + bin_ref[...]
    glu = p[:, :D_MODEL] * _sigmoid(p[:, D_MODEL:])
    rows = lax.broadcasted_iota(jnp.int32, (CONV_HALO + ts, 1), 0)
    glu_ref[...] = jnp.where((rows >= CONV_HALO) | (i > 0), glu, 0.0)
    acc = jnp.zeros((ts, D_MODEL), F32) + cb_ref[...]
    first = CONV_HALO - (CONV_KERNEL - 1)
    for k in range(CONV_KERNEL):
        acc = acc + cw_ref[k:k + 1, :] * glu_ref[pl.ds(first + k, ts), :]
    xc = acc - jnp.mean(acc, axis=-1, keepdims=True)
    y = xc * lax.rsqrt(jnp.mean(xc * xc, axis=-1, keepdims=True) + RMS_EPS)
    y = y * lg_ref[...] + lb_ref[...]
    y_ref[...] = (y * _sigmoid(y)).astype(BF16)


def _conformer(x, g, w_in, b_in, conv_w, conv_b, ln_g, ln_b):
    B, S, _ = x.shape
    ts = min(SEQ_TILE, S)
    per = ts // CONV_HALO
    kern = functools.partial(_conformer_kernel, ts=ts)
    return pl.pallas_call(
        kern,
        out_shape=jax.ShapeDtypeStruct((B, S, D_MODEL), BF16),
        grid=(B, S // ts),
        in_specs=[
            pl.BlockSpec((None, CONV_HALO, D_MODEL), lambda b, i: (b, jnp.maximum(i * per - 1, 0), 0)),
            pl.BlockSpec((None, ts, D_MODEL), lambda b, i: (b, i, 0)),
            _const_spec((1, D_MODEL)),
            _const_spec((D_MODEL, 2 * D_MODEL)),
            _const_spec((1, 2 * D_MODEL)),
            _const_spec((CONV_KERNEL, D_MODEL)),
            _const_spec((1, D_MODEL)),
            _const_spec((1, D_MODEL)),
            _const_spec((1, D_MODEL)),
        ],
        out_specs=pl.BlockSpec((None, ts, D_MODEL), lambda b, i: (b, i, 0)),
        scratch_shapes=[pltpu.VMEM((CONV_HALO + ts, D_MODEL), F32)],
        compiler_params=_params("parallel", "parallel"),
        name="conformer",
    )(x, x, g, w_in, b_in, conv_w, conv_b, ln_g, ln_b)


def _short_conv_kernel(xh_ref, x_ref, g_ref, win_ref, cw_ref, y_ref, u_ref, *, ts):
    i = pl.program_id(1)
    xa = jnp.concatenate([xh_ref[...], x_ref[...]], axis=0)
    hn = _rms(xa, g_ref[...]).astype(BF16)
    p = _dot(hn, win_ref[...])
    u = p[:, D_MODEL:2 * D_MODEL] * p[:, 2 * D_MODEL:]
    rows = lax.broadcasted_iota(jnp.int32, (SHORT_HALO + ts, 1), 0)
    u_ref[...] = jnp.where((rows >= SHORT_HALO) | (i > 0), u, 0.0)
    first = SHORT_HALO - (SHORT_CONV - 1)
    conv = cw_ref[0:1, :] * u_ref[pl.ds(first, ts), :]
    for k in range(1, SHORT_CONV):
        conv = conv + cw_ref[k:k + 1, :] * u_ref[pl.ds(first + k, ts), :]
    y_ref[...] = (p[SHORT_HALO:, :D_MODEL] * conv).astype(BF16)


def _short_conv(x, g, w_in, conv_w):
    B, S, _ = x.shape
    ts = min(SEQ_TILE, S)
    per = ts // SHORT_HALO
    kern = functools.partial(_short_conv_kernel, ts=ts)
    return pl.pallas_call(
        kern,
        out_shape=jax.ShapeDtypeStruct((B, S, D_MODEL), BF16),
        grid=(B, S // ts),
        in_specs=[
            pl.BlockSpec((None, SHORT_HALO, D_MODEL), lambda b, i: (b, jnp.maximum(i * per - 1, 0), 0)),
            pl.BlockSpec((None, ts, D_MODEL), lambda b, i: (b, i, 0)),
            _const_spec((1, D_MODEL)),
            _const_spec((D_MODEL, 3 * D_MODEL)),
            _const_spec((SHORT_CONV, D_MODEL)),
        ],
        out_specs=pl.BlockSpec((None, ts, D_MODEL), lambda b, i: (b, i, 0)),
        scratch_shapes=[pltpu.VMEM((SHORT_HALO + ts, D_MODEL), F32)],
        compiler_params=_params("parallel", "parallel"),
        name="short_conv",
    )(x, x, g, w_in, conv_w)


FOX_SCALE = FOX_HEAD_DIM ** -0.5
N_SPLIT = 3


def _fox_constants():
    H, dh = FOX_HEADS, FOX_HEAD_DIM
    e_down = np.zeros((D_MODEL, LANES), np.float32)
    e_up = np.zeros((LANES, D_MODEL), np.float32)
    for h in range(H):
        e_down[h * dh:(h + 1) * dh, h] = 1.0 / dh
        e_up[h, h * dh:(h + 1) * dh] = 1.0
    p_q = np.zeros((LANES, H * LANES), np.float32)
    p_k = np.zeros((LANES, H * LANES), np.float32)
    one_q = np.zeros((1, H * LANES), np.float32)
    one_k = np.zeros((1, H * LANES), np.float32)
    for h in range(H):
        base = h * LANES + (dh if h % 2 == 0 else 0)
        for s in range(N_SPLIT):
            p_q[s * H + h, base + s] = 1.0
            one_q[0, base + N_SPLIT + s] = 1.0
            one_k[0, base + s] = 1.0
            p_k[s * H + h, base + N_SPLIT + s] = -1.0
    bf = lambda a: jnp.asarray(a, BF16)
    return (bf(np.concatenate([e_down, e_down], 0)), bf(np.concatenate([e_up, e_up], 0)),
            bf(p_q), bf(p_k), jnp.asarray(one_q), jnp.asarray(one_k))


def _head_rms(t, gain, edown2_ref, eup2_ref):
    t2 = t * t
    hi = t2.astype(BF16)
    lo = (t2 - hi.astype(F32)).astype(BF16)
    ms = _dot(jnp.concatenate([hi, lo], axis=1), edown2_ref[...])
    inv = lax.rsqrt(ms + RMS_EPS)
    ihi = inv.astype(BF16)
    ilo = (inv - ihi.astype(F32)).astype(BF16)
    invb = _dot(jnp.concatenate([ihi, ilo], axis=1), eup2_ref[...])
    return t * invb * gain


def _fox_proj_kernel(x_ref, g_ref, wqkv_ref, wf_ref, bf_ref, gq_ref, gk_ref, edown2_ref, eup2_ref,
                     pq_ref, pk_ref, oneq_ref, onek_ref, qa_ref, ka_ref, v_ref, carry_ref, *, ts):
    @pl.when(pl.program_id(1) == 0)
    def _():
        carry_ref[...] = jnp.zeros_like(carry_ref)

    hn = _rms(x_ref[...], g_ref[...]).astype(BF16)
    qkv = _dot(hn, wqkv_ref[...])
    qn = _head_rms(qkv[:, :D_MODEL], gq_ref[...], edown2_ref, eup2_ref)
    kn = _head_rms(qkv[:, D_MODEL:2 * D_MODEL], gk_ref[...], edown2_ref, eup2_ref)
    v_ref[...] = qkv[:, 2 * D_MODEL:].astype(BF16)

    lane = lax.broadcasted_iota(jnp.int32, (1, LANES), 1)
    logf = _log_sigmoid(_dot(hn, wf_ref[...]) + bf_ref[...])
    logf = jnp.where(lane < FOX_HEADS, logf, 0.0)
    tri = (lax.broadcasted_iota(jnp.int32, (ts, ts), 0)
           >= lax.broadcasted_iota(jnp.int32, (ts, ts), 1)).astype(BF16)
    f1, f2, f3 = _split3(logf)
    c = (_dot(tri, f1) + _dot(tri, f2)) + _dot(tri, f3) + carry_ref[...]
    carry_ref[...] = c[ts - 1:ts, :]
    c1, c2, c3 = _split3(c)
    packed = (c1.astype(F32) + pltpu.roll(c2.astype(F32), FOX_HEADS, 1)
              + pltpu.roll(c3.astype(F32), 2 * FOX_HEADS, 1)).astype(BF16)
    aug_q = _dot(packed, pq_ref[...]) + oneq_ref[...]
    aug_k = _dot(packed, pk_ref[...]) + onek_ref[...]
    low = lane < FOX_HEAD_DIM
    for h in range(FOX_HEADS):
        own = low if h % 2 == 0 else jnp.logical_not(low)
        blk = slice((h // 2) * LANES, (h // 2 + 1) * LANES)
        aug = slice(h * LANES, (h + 1) * LANES)
        qa_ref[h] = (aug_q[:, aug] + jnp.where(own, qn[:, blk], 0.0)).astype(BF16)
        ka_ref[h] = (aug_k[:, aug] + jnp.where(own, kn[:, blk], 0.0)).astype(BF16)


def _fox_proj(x, g, w_qkv, w_f, b_f, gq, gk):
    B, S, _ = x.shape
    ts = min(SEQ_TILE, S)
    consts = _fox_constants()
    kern = functools.partial(_fox_proj_kernel, ts=ts)
    aug_shape = jax.ShapeDtypeStruct((B, FOX_HEADS, S, LANES), BF16)
    aug_spec = pl.BlockSpec((None, FOX_HEADS, ts, LANES), lambda b, i: (b, 0, i, 0))
    tile = pl.BlockSpec((None, ts, D_MODEL), lambda b, i: (b, i, 0))
    return pl.pallas_call(
        kern,
        out_shape=(aug_shape, aug_shape, jax.ShapeDtypeStruct((B, S, D_MODEL), BF16)),
        grid=(B, S // ts),
        in_specs=[tile, _const_spec((1, D_MODEL)), _const_spec((D_MODEL, 3 * D_MODEL)),
                  _const_spec((D_MODEL, LANES)), _const_spec((1, LANES)),
                  _const_spec((1, D_MODEL)), _const_spec((1, D_MODEL))]
                 + [_const_spec(c.shape) for c in consts],
        out_specs=(aug_spec, aug_spec, tile),
        scratch_shapes=[pltpu.VMEM((1, LANES), F32)],
        compiler_params=_params("parallel", "arbitrary"),
        name="fox_proj",
    )(x, g, w_qkv, w_f, b_f, gq, gk, *consts)


FOX_TQ = 512


def _fox_attn_kernel(q_ref, k_ref, v_ref, o_ref, *, tq):
    i = pl.program_id(2)
    low = lax.broadcasted_iota(jnp.int32, (1, LANES), 1) < FOX_HEAD_DIM
    causal = (lax.broadcasted_iota(jnp.int32, (tq, tq), 1)
              <= lax.broadcasted_iota(jnp.int32, (tq, tq), 0))
    qs = (q_ref[0], q_ref[1])

    def step(j, carry, diagonal):
        off = pl.multiple_of(j * tq, tq)
        vblk = v_ref[pl.ds(off, tq), :]
        out = []
        for h in range(2):
            m, l, a = carry[h]
            s = lax.dot_general(qs[h], k_ref[h, pl.ds(off, tq), :], _NT, preferred_element_type=F32)
            if diagonal:
                s = jnp.where(causal, s, NEG_BIG)
            m_new = jnp.maximum(m, jnp.max(s, axis=-1, keepdims=True))
            alpha = jnp.exp(m - m_new)
            p = jnp.exp(s - m_new)
            l = alpha * l + jnp.sum(p, axis=-1, keepdims=True)
            a = alpha * a + _dot(p.astype(BF16), vblk)
            out.append((m_new, l, a))
        return tuple(out)

    init = tuple((jnp.full((tq, 1), NEG_BIG, F32), jnp.zeros((tq, 1), F32),
                  jnp.zeros((tq, LANES), F32)) for _ in range(2))
    carry = lax.fori_loop(0, i, lambda j, c: step(j, c, False), init)
    (_, l0, a0), (_, l1, a1) = step(i, carry, True)
    o_ref[...] = jnp.where(low, a0 / l0, a1 / l1).astype(BF16)


def _fox_attn(qa, ka, v):
    B, H, S, _ = qa.shape
    tq = min(FOX_TQ, S)
    kern = functools.partial(_fox_attn_kernel, tq=tq)
    return pl.pallas_call(
        kern,
        out_shape=jax.ShapeDtypeStruct((B, S, D_MODEL), BF16),
        grid=(B, H // 2, S // tq),
        in_specs=[
            pl.BlockSpec((None, 2, tq, LANES), lambda b, p, i: (b, p, i, 0)),
            pl.BlockSpec((None, 2, S, LANES), lambda b, p, i: (b, p, 0, 0)),
            pl.BlockSpec((None, S, LANES), lambda b, p, i: (b, 0, p)),
        ],
        out_specs=pl.BlockSpec((None, tq, LANES), lambda b, p, i: (b, i, p)),
        compiler_params=_params("parallel", "parallel", "arbitrary"),
        name="fox_attn",
    )(qa, ka, v)


GLA_SCALE = GLA_DK_HEAD ** -0.5


def _gla_kernel(x_ref, g_ref, win_ref, wg1_ref, wg2_ref, bg_ref, og_ref, y_ref, st_ref, *, ts):
    @pl.when(pl.program_id(1) == 0)
    def _():
        st_ref[...] = jnp.zeros_like(st_ref)

    C = GLA_CHUNK
    hn = _rms(x_ref[...], g_ref[...]).astype(BF16)
    proj = _dot(hn, win_ref[...])
    gate = _dot(_dot(hn, wg1_ref[...]).astype(BF16), wg2_ref[...]) + bg_ref[...]
    log_a = _log_sigmoid(gate) / GLA_TAU
    tri = (lax.broadcasted_iota(jnp.int32, (C, C), 0)
           >= lax.broadcasted_iota(jnp.int32, (C, C), 1))
    tri3 = jnp.concatenate([tri.astype(BF16)] * N_SPLIT, axis=1)
    og = og_ref[...]
    for c in range(ts // C):
        rows = slice(c * C, (c + 1) * C)
        b = _dot(tri3, jnp.concatenate(_split3(log_a[rows]), axis=0))
        b_last = b[C - 1:C, :]
        q_t = proj[rows, :GLA_DK] * GLA_SCALE * jnp.exp(b)
        kc = proj[rows, GLA_DK:2 * GLA_DK]
        k_t = kc * jnp.exp(-b)
        k_dec = kc * jnp.exp(b_last - b)
        decay = jnp.exp(b_last)
        for h in range(GLA_HEADS):
            dk = slice(h * GLA_DK_HEAD, (h + 1) * GLA_DK_HEAD)
            dv = slice(h * GLA_DV_HEAD, (h + 1) * GLA_DV_HEAD)
            qh = q_t[:, dk].astype(BF16)
            vh = proj[rows, 2 * GLA_DK + h * GLA_DV_HEAD:2 * GLA_DK + (h + 1) * GLA_DV_HEAD].astype(BF16)
            att = lax.dot_general(qh, k_t[:, dk].astype(BF16), _NT, preferred_element_type=F32)
            att = jnp.where(tri, att, 0.0)
            st = st_ref[h]
            o = _dot(att.astype(BF16), vh) + lax.dot_general(qh, st.astype(BF16), _NT,
                                                             preferred_element_type=F32)
            st_ref[h] = st * decay[:, dk] + lax.dot_general(vh, k_dec[:, dk].astype(BF16), _TN,
                                                            preferred_element_type=F32)
            on = o * lax.rsqrt(jnp.mean(o * o, axis=-1, keepdims=True) + RMS_EPS) * og
            r = proj[rows, 2 * GLA_DK + GLA_DV + h * GLA_DV_HEAD:2 * GLA_DK + GLA_DV + (h + 1) * GLA_DV_HEAD]
            y_ref[rows, dv] = (on * (r * _sigmoid(r))).astype(BF16)


def _gla(x, g, w_in, w_g1, w_g2, b_g, o_g):
    B, S, _ = x.shape
    ts = min(SEQ_TILE, S)
    kern = functools.partial(_gla_kernel, ts=ts)
    tile = pl.BlockSpec((None, ts, D_MODEL), lambda b, i: (b, i, 0))
    return pl.pallas_call(
        kern,
        out_shape=jax.ShapeDtypeStruct((B, S, D_MODEL), BF16),
        grid=(B, S // ts),
        in_specs=[tile, _const_spec((1, D_MODEL)), _const_spec((D_MODEL, 2 * GLA_DK + 2 * GLA_DV)),
                  _const_spec((D_MODEL, LANES)), _const_spec((LANES, GLA_DK)),
                  _const_spec((1, GLA_DK)), _const_spec((1, GLA_DV_HEAD))],
        out_specs=tile,
        scratch_shapes=[pltpu.VMEM((GLA_HEADS, GLA_DV_HEAD, GLA_DK_HEAD), F32)],
        compiler_params=_params("parallel", "arbitrary"),
        name="gla",
    )(x, g, w_in, w_g1, w_g2, b_g, o_g)


def _row(v):
    return v.reshape(1, -1).astype(F32)


def _pad_cols(w, n):
    return jnp.pad(w, ((0, 0), (0, n - w.shape[1])))


def kernel(x, mix_norm, mlp_norm, mlp_w1, mlp_w2,
           a_w_in, a_b_in, a_conv_w, a_conv_b, a_ln_g, a_ln_b, a_w_out, a_b_out,
           b_w_in, b_conv_w, b_w_out,
           c_w_qkv, c_q_norm, c_k_norm, c_w_f, c_b_f, c_w_out,
           d_w_in, d_w_g1, d_w_g2, d_b_g, d_o_norm, d_w_out):
    B, S, _ = x.shape
    depth = mix_norm.shape[0]
    zero_bias = jnp.zeros((1, D_MODEL), F32)
    for i in range(depth):
        m, j = i % 4, i // 4
        g = _row(mix_norm[i])
        if m == 0:
            y = _conformer(x, g, a_w_in[j].astype(BF16), _row(a_b_in[j]), a_conv_w[j], _row(a_conv_b[j]),
                           _row(a_ln_g[j]), _row(a_ln_b[j]))
            wo, bo = a_w_out[j], _row(a_b_out[j])
        elif m == 1:
            y = _short_conv(x, g, b_w_in[j].astype(BF16), b_conv_w[j])
            wo, bo = b_w_out[j], zero_bias
        elif m == 2:
            gq = _row(jnp.tile(c_q_norm[j], FOX_HEADS)) * FOX_SCALE
            gk = _row(jnp.tile(c_k_norm[j], FOX_HEADS))
            qa, ka, v = _fox_proj(x, g, c_w_qkv[j].astype(BF16), _pad_cols(c_w_f[j], LANES).astype(BF16),
                                  _pad_cols(_row(c_b_f[j]), LANES), gq, gk)
            y = _fox_attn(qa, ka, v)
            wo, bo = c_w_out[j], zero_bias
        else:
            w_g2 = jnp.pad(d_w_g2[j], ((0, LANES - GLA_GATE_RANK), (0, 0)))
            y = _gla(x, g, d_w_in[j].astype(BF16), _pad_cols(d_w_g1[j], LANES).astype(BF16),
                     w_g2.astype(BF16), _row(d_b_g[j]), _row(d_o_norm[j]))
            wo, bo = d_w_out[j], zero_bias
        x = _proj_mlp(x.reshape(B * S, D_MODEL), y.reshape(B * S, D_MODEL), wo.astype(BF16), bo,
                      _row(mlp_norm[i]), mlp_w1[i].astype(BF16), mlp_w2[i].astype(BF16)).reshape(B, S, D_MODEL)
    return x
```

```python
import functools

import numpy as np
import jax
import jax.numpy as jnp
from jax import lax
from jax.experimental import pallas as pl
from jax.experimental.pallas import tpu as pltpu

F32 = jnp.float32
BF16 = jnp.bfloat16

D_MODEL = 1024
D_FF = 4 * D_MODEL
RMS_EPS = 1e-6
CONV_KERNEL = 31
SHORT_CONV = 3
FOX_HEADS = 16
FOX_HEAD_DIM = D_MODEL // FOX_HEADS
GLA_HEADS = 4
GLA_DK = D_MODEL // 2
GLA_DV = D_MODEL
GLA_DK_HEAD = GLA_DK // GLA_HEADS
GLA_DV_HEAD = GLA_DV // GLA_HEADS
GLA_GATE_RANK = 16
GLA_TAU = 16.0
GLA_CHUNK = 64

LANES = 128
SUBLANES = 8
CONV_ROWS = 64
CONV_COLS = 256
CONV_HALO = 32
SHORT_HALO = 8
NEG_BIG = -1e30
VMEM_LIMIT = 56 * 1024 * 1024

_NT = (((1,), (1,)), ((), ()))
_TN = (((0,), (0,)), ((), ()))


def _dot(a, b):
    return jnp.dot(a, b, preferred_element_type=F32)


def _rms(x, g):
    return x * lax.rsqrt(jnp.mean(x * x, axis=-1, keepdims=True) + RMS_EPS) * g


def _sigmoid(x):
    return 1.0 / (1.0 + jnp.exp(-x))


def _log_sigmoid(x):
    return jnp.minimum(x, 0.0) - jnp.log1p(jnp.exp(-jnp.abs(x)))


def _split3(x):
    h1 = x.astype(BF16)
    r1 = x - h1.astype(F32)
    h2 = r1.astype(BF16)
    h3 = (r1 - h2.astype(F32)).astype(BF16)
    return h1, h2, h3


def _const_spec(shape):
    nd = len(shape)
    return pl.BlockSpec(shape, lambda *_: (0,) * nd, pipeline_mode=pl.Buffered(1))


def _params(*sem):
    return pltpu.CompilerParams(dimension_semantics=sem, vmem_limit_bytes=VMEM_LIMIT)


MLP_TM = 512
MLP_TF = 1024


def _mlp_kernel(x_ref, y_ref, wo_ref, bo_ref, g_ref, w1_ref, w2_ref, o_ref):
    x1 = x_ref[...] + _dot(y_ref[...], wo_ref[...]) + bo_ref[...]
    hn = _rms(x1, g_ref[...]).astype(BF16)
    acc = x1
    for c in range(D_FF // MLP_TF):
        cols = slice(c * MLP_TF, (c + 1) * MLP_TF)
        h = jnp.maximum(_dot(hn, w1_ref[:, cols]), 0.0)
        acc = acc + _dot((h * h).astype(BF16), w2_ref[cols, :])
    o_ref[...] = acc


def _proj_mlp(x2d, y2d, wo, bo, g, w1, w2):
    T = x2d.shape[0]
    tm = min(MLP_TM, T)
    row = lambda i: (i, 0)
    return pl.pallas_call(
        _mlp_kernel,
        out_shape=jax.ShapeDtypeStruct((T, D_MODEL), F32),
        grid=(T // tm,),
        in_specs=[
            pl.BlockSpec((tm, D_MODEL), row),
            pl.BlockSpec((tm, D_MODEL), row),
            _const_spec((D_MODEL, D_MODEL)),
            _const_spec((1, D_MODEL)),
            _const_spec((1, D_MODEL)),
            _const_spec((D_MODEL, D_FF)),
            _const_spec((D_FF, D_MODEL)),
        ],
        out_specs=pl.BlockSpec((tm, D_MODEL), row),
        compiler_params=_params("parallel"),
        name="proj_mlp",
    )(x2d, y2d, wo, bo, g, w1, w2)


SEQ_TILE = 512


def _conformer_kernel(xh_ref, x_ref, g_ref, win_ref, bin_ref, cw_ref, cb_ref, lg_ref, lb_ref,
                      y_ref, glu_ref, sh_ref, *, ts):
    i = pl.program_id(1)
    xa = jnp.concatenate([xh_ref[...], x_ref[...]], axis=0)
    hn = _rms(xa, g_ref[...]).astype(BF16)
    p = _dot(hn, win_ref[...]) + bin_ref[...]
    glu = p[:, :D_MODEL] * _sigmoid(p[:, D_MODEL:])
    rows = lax.broadcasted_iota(jnp.int32, (CONV_HALO + ts, 1), 0)
    glu = jnp.where((rows >= CONV_HALO) | (i > 0), glu, 0.0)
    glu_ref[...] = glu
    span = ts + CONV_HALO - SUBLANES
    for r in range(1, SUBLANES):
        sh_ref[r - 1] = pltpu.roll(glu, CONV_HALO + ts - r, 0)[:span]
    first = CONV_HALO - (CONV_KERNEL - 1)
    taps = {}
    for k in range(CONV_KERNEL):
        a, r = divmod(first + k, SUBLANES)
        taps.setdefault(r, []).append((k, a * SUBLANES))

    def row_chunk(rc, carry):
        r0 = pl.multiple_of(rc * CONV_ROWS, CONV_ROWS)
        pieces = []
        for cb in range(D_MODEL // CONV_COLS):
            cols = slice(cb * CONV_COLS, (cb + 1) * CONV_COLS)
            acc = jnp.broadcast_to(cb_ref[:, cols], (CONV_ROWS, CONV_COLS))
            for r, group in taps.items():
                at = pl.ds(r0, CONV_ROWS + group[-1][1])
                win = glu_ref[at, cols] if r == 0 else sh_ref[r - 1, at, cols]
                for k, off in group:
                    acc = acc + cw_ref[k:k + 1, cols] * win[off:off + CONV_ROWS]
            pieces.append(acc)
        acc = jnp.concatenate(pieces, axis=1)
        xc = acc - jnp.mean(acc, axis=-1, keepdims=True)
        y = xc * lax.rsqrt(jnp.mean(xc * xc, axis=-1, keepdims=True) + RMS_EPS)
        y = y * lg_ref[...] + lb_ref[...]
        y_ref[pl.ds(r0, CONV_ROWS), :] = (y * _sigmoid(y)).astype(BF16)
        return carry

    lax.fori_loop(0, ts // CONV_ROWS, row_chunk, 0)


def _conformer(x, g, w_in, b_in, conv_w, conv_b, ln_g, ln_b):
    B, S, _ = x.shape
    ts = min(SEQ_TILE, S)
    per = ts // CONV_HALO
    kern = functools.partial(_conformer_kernel, ts=ts)
    return pl.pallas_call(
        kern,
        out_shape=jax.ShapeDtypeStruct((B, S, D_MODEL), BF16),
        grid=(B, S // ts),
        in_specs=[
            pl.BlockSpec((None, CONV_HALO, D_MODEL), lambda b, i: (b, jnp.maximum(i * per - 1, 0), 0)),
            pl.BlockSpec((None, ts, D_MODEL), lambda b, i: (b, i, 0)),
            _const_spec((1, D_MODEL)),
            _const_spec((D_MODEL, 2 * D_MODEL)),
            _const_spec((1, 2 * D_MODEL)),
            _const_spec((CONV_KERNEL, D_MODEL)),
            _const_spec((1, D_MODEL)),
            _const_spec((1, D_MODEL)),
            _const_spec((1, D_MODEL)),
        ],
        out_specs=pl.BlockSpec((None, ts, D_MODEL), lambda b, i: (b, i, 0)),
        scratch_shapes=[pltpu.VMEM((CONV_HALO + ts, D_MODEL), F32),
                        pltpu.VMEM((SUBLANES - 1, ts + CONV_HALO - SUBLANES, D_MODEL), F32)],
        compiler_params=_params("parallel", "parallel"),
        name="conformer",
    )(x, x, g, w_in, b_in, conv_w, conv_b, ln_g, ln_b)


def _short_conv_kernel(xh_ref, x_ref, g_ref, win_ref, cw_ref, y_ref, u_ref, *, ts):
    i = pl.program_id(1)
    xa = jnp.concatenate([xh_ref[...], x_ref[...]], axis=0)
    hn = _rms(xa, g_ref[...]).astype(BF16)
    p = _dot(hn, win_ref[...])
    u = p[:, D_MODEL:2 * D_MODEL] * p[:, 2 * D_MODEL:]
    rows = lax.broadcasted_iota(jnp.int32, (SHORT_HALO + ts, 1), 0)
    u_ref[...] = jnp.where((rows >= SHORT_HALO) | (i > 0), u, 0.0)
    first = SHORT_HALO - (SHORT_CONV - 1)
    conv = cw_ref[0:1, :] * u_ref[pl.ds(first, ts), :]
    for k in range(1, SHORT_CONV):
        conv = conv + cw_ref[k:k + 1, :] * u_ref[pl.ds(first + k, ts), :]
    y_ref[...] = (p[SHORT_HALO:, :D_MODEL] * conv).astype(BF16)


def _short_conv(x, g, w_in, conv_w):
    B, S, _ = x.shape
    ts = min(SEQ_TILE, S)
    per = ts // SHORT_HALO
    kern = functools.partial(_short_conv_kernel, ts=ts)
    return pl.pallas_call(
        kern,
        out_shape=jax.ShapeDtypeStruct((B, S, D_MODEL), BF16),
        grid=(B, S // ts),
        in_specs=[
            pl.BlockSpec((None, SHORT_HALO, D_MODEL), lambda b, i: (b, jnp.maximum(i * per - 1, 0), 0)),
            pl.BlockSpec((None, ts, D_MODEL), lambda b, i: (b, i, 0)),
            _const_spec((1, D_MODEL)),
            _const_spec((D_MODEL, 3 * D_MODEL)),
            _const_spec((SHORT_CONV, D_MODEL)),
        ],
        out_specs=pl.BlockSpec((None, ts, D_MODEL), lambda b, i: (b, i, 0)),
        scratch_shapes=[pltpu.VMEM((SHORT_HALO + ts, D_MODEL), F32)],
        compiler_params=_params("parallel", "parallel"),
        name="short_conv",
    )(x, x, g, w_in, conv_w)


FOX_SCALE = FOX_HEAD_DIM ** -0.5
LOG2E = 1.4426950408889634
N_SPLIT = 3
FOX_TQ = 512
FOX_TQS = 256
FOX_TK = 256


def _fox_constants():
    H, dh = FOX_HEADS, FOX_HEAD_DIM
    p_q = np.zeros((LANES, D_MODEL), np.float32)
    p_k = np.zeros((LANES, D_MODEL), np.float32)
    one_q = np.zeros((1, D_MODEL), np.float32)
    one_k = np.zeros((1, D_MODEL), np.float32)
    for h in range(H):
        base = (h // 2) * LANES + (dh if h % 2 == 0 else 0)
        for s in range(N_SPLIT):
            p_q[s * H + h, base + s] = 1.0
            one_q[0, base + N_SPLIT + s] = 1.0
            one_k[0, base + s] = 1.0
            p_k[s * H + h, base + N_SPLIT + s] = -1.0
    return jnp.asarray(p_q, BF16), jnp.asarray(p_k, BF16), jnp.asarray(one_q), jnp.asarray(one_k)


def _pack_pieces(pieces):
    out = pieces[0].astype(F32)
    for s in range(1, N_SPLIT):
        out = out + pltpu.roll(pieces[s].astype(F32), s * FOX_HEADS, 1)
    return out.astype(BF16)


def _pair_rms(blk, gain, low):
    sq = blk * blk
    s_low = jnp.sum(jnp.where(low, sq, 0.0), axis=-1, keepdims=True)
    s_high = jnp.sum(jnp.where(low, 0.0, sq), axis=-1, keepdims=True)
    inv = jnp.where(low, lax.rsqrt(s_low * (1.0 / FOX_HEAD_DIM) + RMS_EPS),
                    lax.rsqrt(s_high * (1.0 / FOX_HEAD_DIM) + RMS_EPS))
    return blk * inv * gain


def _fox_proj_kernel(x_ref, g_ref, wqkv_ref, wf_ref, bf_ref, gq_ref, gk_ref,
                     pq_ref, pk_ref, oneq_ref, onek_ref, qt_ref, ka_ref, vt_ref, carry_ref, *, ts, tk):
    @pl.when(pl.program_id(1) == 0)
    def _():
        carry_ref[...] = jnp.zeros_like(carry_ref)

    hn = _rms(x_ref[...], g_ref[...]).astype(BF16)
    qkv = _dot(hn, wqkv_ref[...])

    lane = lax.broadcasted_iota(jnp.int32, (1, LANES), 1)
    logf = _log_sigmoid(_dot(hn, wf_ref[...]) + bf_ref[...])
    logf = jnp.where(lane < FOX_HEADS, logf, 0.0)
    tri = (lax.broadcasted_iota(jnp.int32, (ts, ts), 0)
           >= lax.broadcasted_iota(jnp.int32, (ts, ts), 1)).astype(BF16)
    cs = _dot(tri, _pack_pieces(_split3(logf)))
    c = cs + pltpu.roll(cs, LANES - FOX_HEADS, 1) + pltpu.roll(cs, LANES - 2 * FOX_HEADS, 1)
    c = jnp.where(lane < FOX_HEADS, c, 0.0) + carry_ref[...]
    carry_ref[...] = c[ts - 1:ts, :]
    packed = _pack_pieces(_split3(c * LOG2E))
    aug_q = _dot(packed, pq_ref[...]) + oneq_ref[...]
    aug_k = _dot(packed, pk_ref[...]) + onek_ref[...]

    low = lane < FOX_HEAD_DIM
    for p in range(FOX_HEADS // 2):
        blk = slice(p * LANES, (p + 1) * LANES)
        qn = _pair_rms(qkv[:, blk], gq_ref[:, blk], low)
        kn = _pair_rms(qkv[:, D_MODEL + p * LANES:D_MODEL + (p + 1) * LANES], gk_ref[:, blk], low)
        qt_ref[2 * p] = jnp.where(low, qn, aug_q[:, blk]).T.astype(BF16)
        qt_ref[2 * p + 1] = jnp.where(low, aug_q[:, blk], qn).T.astype(BF16)
        ka_ref[2 * p] = jnp.where(low, kn, aug_k[:, blk]).astype(BF16)
        ka_ref[2 * p + 1] = jnp.where(low, aug_k[:, blk], kn).astype(BF16)
        vt = qkv[:, 2 * D_MODEL + p * LANES:2 * D_MODEL + (p + 1) * LANES].T
        for jj in range(ts // tk):
            vt_ref[p, jj] = vt[:, jj * tk:(jj + 1) * tk].astype(BF16)


def _fox_proj(x, g, w_qkv, w_f, b_f, gq, gk):
    B, S, _ = x.shape
    ts = min(SEQ_TILE, S)
    tk = min(FOX_TK, S)
    consts = _fox_constants()
    kern = functools.partial(_fox_proj_kernel, ts=ts, tk=tk)
    tile = pl.BlockSpec((None, ts, D_MODEL), lambda b, i: (b, i, 0))
    return pl.pallas_call(
        kern,
        out_shape=(jax.ShapeDtypeStruct((B, FOX_HEADS, LANES, S), BF16),
                   jax.ShapeDtypeStruct((B, FOX_HEADS, S, LANES), BF16),
                   jax.ShapeDtypeStruct((B, FOX_HEADS // 2, S // tk, LANES, tk), BF16)),
        grid=(B, S // ts),
        in_specs=[tile, _const_spec((1, D_MODEL)), _const_spec((D_MODEL, 3 * D_MODEL)),
                  _const_spec((D_MODEL, LANES)), _const_spec((1, LANES)),
                  _const_spec((1, D_MODEL)), _const_spec((1, D_MODEL))]
                 + [_const_spec(c.shape) for c in consts],
        out_specs=(pl.BlockSpec((None, FOX_HEADS, LANES, ts), lambda b, i: (b, 0, 0, i)),
                   pl.BlockSpec((None, FOX_HEADS, ts, LANES), lambda b, i: (b, 0, i, 0)),
                   pl.BlockSpec((None, FOX_HEADS // 2, ts // tk, LANES, tk), lambda b, i: (b, 0, i, 0, 0))),
        scratch_shapes=[pltpu.VMEM((1, LANES), F32)],
        compiler_params=_params("parallel", "arbitrary"),
        name="fox_proj",
    )(x, g, w_qkv, w_f, b_f, gq, gk, *consts)


def _fox_attn_kernel(qt_ref, k_ref, vt_ref, o_ref, st_ref, *, tq, tqs, tk):
    i = pl.program_id(2)
    nsub = tq // tqs
    nkb = tq // tk
    dh = FOX_HEAD_DIM
    chains = [(h, qs) for h in range(2) for qs in range(nsub)]

    def scores(n, jblk):
        h, qs = chains[n]
        koff = pl.multiple_of(jblk * tk, tk)
        return _dot(k_ref[h, pl.ds(koff, tk), :], qt_ref[h, :, qs * tqs:(qs + 1) * tqs])

    def softmax_step(st, state):
        m, l, a = state
        m_new = jnp.maximum(m, jnp.max(st, axis=0, keepdims=True))
        alpha = jnp.exp2(m - m_new)
        p = jnp.exp2(st - m_new)
        return m_new, alpha * l + jnp.sum(p, axis=0, keepdims=True), alpha, p.astype(BF16)

    def values(n, jblk, alpha, p, a):
        h, _ = chains[n]
        return alpha * a + _dot(vt_ref[jblk, h * dh:(h + 1) * dh, :], p)

    nch = len(chains)

    def block(jblk, slot, states, masks, next_masks):
        cur = slot * nch
        nxt = nch - cur
        out = list(states)
        for n in range(nch):
            if next_masks is not None and next_masks[n] is not False:
                st_ref[nxt + n] = scores(n, jblk + 1)
            if masks[n] is False:
                continue
            st = st_ref[cur + n]
            if masks[n] is not None:
                st = jnp.where(masks[n], st, NEG_BIG)
            m_new, l, alpha, p = softmax_step(st, states[n])
            out[n] = (m_new, l, values(n, jblk, alpha, p, states[n][2]))
        return tuple(out)

    row = lax.broadcasted_iota(jnp.int32, (tk, tqs), 0)
    col = lax.broadcasted_iota(jnp.int32, (tk, tqs), 1)
    diag_masks = []
    for kb in range(nkb):
        masks = []
        for _, qs in chains:
            k_lo, q_lo = kb * tk, qs * tqs
            if k_lo > q_lo + tqs - 1:
                masks.append(False)
            else:
                masks.append(None if k_lo + tk - 1 <= q_lo else (row + k_lo <= col + q_lo))
        diag_masks.append(masks)
    no_mask = [None] * len(chains)

    states = tuple((jnp.full((1, tqs), NEG_BIG, F32), jnp.zeros((1, tqs), F32), jnp.zeros((dh, tqs), F32))
                   for _ in chains)
    for n in range(nch):
        st_ref[n] = scores(n, 0)
    def two_blocks(jj, c):
        c = block(2 * jj, 0, c, no_mask, no_mask)
        return block(2 * jj + 1, 1, c, no_mask, no_mask)

    states = lax.fori_loop(0, i * (nkb // 2), two_blocks, states)
    for kb in range(nkb):
        states = block(i * nkb + kb, kb % 2, states, diag_masks[kb], diag_masks[kb + 1] if kb + 1 < nkb else None)
    out_t = jnp.concatenate(
        [jnp.concatenate([states[h * nsub + qs][2] / states[h * nsub + qs][1] for qs in range(nsub)], axis=1)
         for h in range(2)], axis=0)
    o_ref[...] = out_t.T.astype(BF16)


def _fox_attn(qt, ka, vt):
    B, H, _, S = qt.shape
    tq, tqs, tk = min(FOX_TQ, S), min(FOX_TQS, S), min(FOX_TK, S)
    kern = functools.partial(_fox_attn_kernel, tq=tq, tqs=tqs, tk=tk)
    return pl.pallas_call(
        kern,
        out_shape=jax.ShapeDtypeStruct((B, S, D_MODEL), BF16),
        grid=(B, H // 2, S // tq),
        in_specs=[
            pl.BlockSpec((None, 2, LANES, tq), lambda b, p, i: (b, p, 0, i)),
            pl.BlockSpec((None, 2, S, LANES), lambda b, p, i: (b, p, 0, 0)),
            pl.BlockSpec((None, None, S // tk, LANES, tk), lambda b, p, i: (b, p, 0, 0, 0)),
        ],
        out_specs=pl.BlockSpec((None, tq, LANES), lambda b, p, i: (b, i, p)),
        scratch_shapes=[pltpu.VMEM((2 * 2 * (tq // tqs), tk, tqs), F32)],
        compiler_params=_params("parallel", "parallel", "arbitrary"),
        name="fox_attn",
    )(qt, ka, vt)


GLA_SCALE = GLA_DK_HEAD ** -0.5


def _gla_kernel(x_ref, g_ref, win_ref, wg1_ref, wg2_ref, bg_ref, og_ref, y_ref, st_ref, *, ts):
    @pl.when(pl.program_id(1) == 0)
    def _():
        st_ref[...] = jnp.zeros_like(st_ref)

    C = GLA_CHUNK
    hn = _rms(x_ref[...], g_ref[...]).astype(BF16)
    proj = _dot(hn, win_ref[...])
    gate = _dot(_dot(hn, wg1_ref[...]).astype(BF16), wg2_ref[...]) + bg_ref[...]
    log_a = _log_sigmoid(gate) / GLA_TAU
    tri = (lax.broadcasted_iota(jnp.int32, (C, C), 0)
           >= lax.broadcasted_iota(jnp.int32, (C, C), 1))
    tri3 = jnp.concatenate([tri.astype(BF16)] * N_SPLIT, axis=1)
    og = og_ref[...]
    for c in range(ts // C):
        rows = slice(c * C, (c + 1) * C)
        b = _dot(tri3, jnp.concatenate(_split3(log_a[rows]), axis=0))
        b_last = b[C - 1:C, :]
        q_t = proj[rows, :GLA_DK] * GLA_SCALE * jnp.exp(b)
        kc = proj[rows, GLA_DK:2 * GLA_DK]
        k_t = kc * jnp.exp(-b)
        k_dec = kc * jnp.exp(b_last - b)
        decay = jnp.exp(b_last)
        for h in range(GLA_HEADS):
            dk = slice(h * GLA_DK_HEAD, (h + 1) * GLA_DK_HEAD)
            dv = slice(h * GLA_DV_HEAD, (h + 1) * GLA_DV_HEAD)
            qh = q_t[:, dk].astype(BF16)
            vh = proj[rows, 2 * GLA_DK + h * GLA_DV_HEAD:2 * GLA_DK + (h + 1) * GLA_DV_HEAD].astype(BF16)
            att = lax.dot_general(qh, k_t[:, dk].astype(BF16), _NT, preferred_element_type=F32)
            att = jnp.where(tri, att, 0.0)
            st = st_ref[h]
            o = _dot(att.astype(BF16), vh) + lax.dot_general(qh, st.astype(BF16), _NT,
                                                             preferred_element_type=F32)
            st_ref[h] = st * decay[:, dk] + lax.dot_general(vh, k_dec[:, dk].astype(BF16), _TN,
                                                            preferred_element_type=F32)
            on = o * lax.rsqrt(jnp.mean(o * o, axis=-1, keepdims=True) + RMS_EPS) * og
            r = proj[rows, 2 * GLA_DK + GLA_DV + h * GLA_DV_HEAD:2 * GLA_DK + GLA_DV + (h + 1) * GLA_DV_HEAD]
            y_ref[rows, dv] = (on * (r * _sigmoid(r))).astype(BF16)


def _gla(x, g, w_in, w_g1, w_g2, b_g, o_g):
    B, S, _ = x.shape
    ts = min(SEQ_TILE, S)
    kern = functools.partial(_gla_kernel, ts=ts)
    tile = pl.BlockSpec((None, ts, D_MODEL), lambda b, i: (b, i, 0))
    return pl.pallas_call(
        kern,
        out_shape=jax.ShapeDtypeStruct((B, S, D_MODEL), BF16),
        grid=(B, S // ts),
        in_specs=[tile, _const_spec((1, D_MODEL)), _const_spec((D_MODEL, 2 * GLA_DK + 2 * GLA_DV)),
                  _const_spec((D_MODEL, LANES)), _const_spec((LANES, GLA_DK)),
                  _const_spec((1, GLA_DK)), _const_spec((1, GLA_DV_HEAD))],
        out_specs=tile,
        scratch_shapes=[pltpu.VMEM((GLA_HEADS, GLA_DV_HEAD, GLA_DK_HEAD), F32)],
        compiler_params=_params("parallel", "arbitrary"),
        name="gla",
    )(x, g, w_in, w_g1, w_g2, b_g, o_g)


def _row(v):
    return v.reshape(1, -1).astype(F32)


def _pad_cols(w, n):
    return jnp.pad(w, ((0, 0), (0, n - w.shape[1])))


def kernel(x, mix_norm, mlp_norm, mlp_w1, mlp_w2,
           a_w_in, a_b_in, a_conv_w, a_conv_b, a_ln_g, a_ln_b, a_w_out, a_b_out,
           b_w_in, b_conv_w, b_w_out,
           c_w_qkv, c_q_norm, c_k_norm, c_w_f, c_b_f, c_w_out,
           d_w_in, d_w_g1, d_w_g2, d_b_g, d_o_norm, d_w_out):
    B, S, _ = x.shape
    depth = mix_norm.shape[0]
    zero_bias = jnp.zeros((1, D_MODEL), F32)
    for i in range(depth):
        m, j = i % 4, i // 4
        g = _row(mix_norm[i])
        if m == 0:
            y = _conformer(x, g, a_w_in[j].astype(BF16), _row(a_b_in[j]), a_conv_w[j], _row(a_conv_b[j]),
                           _row(a_ln_g[j]), _row(a_ln_b[j]))
            wo, bo = a_w_out[j], _row(a_b_out[j])
        elif m == 1:
            y = _short_conv(x, g, b_w_in[j].astype(BF16), b_conv_w[j])
            wo, bo = b_w_out[j], zero_bias
        elif m == 2:
            gq = _row(jnp.tile(c_q_norm[j], FOX_HEADS)) * (FOX_SCALE * LOG2E)
            gk = _row(jnp.tile(c_k_norm[j], FOX_HEADS))
            qt, ka, vt = _fox_proj(x, g, c_w_qkv[j].astype(BF16), _pad_cols(c_w_f[j], LANES).astype(BF16),
                                  _pad_cols(_row(c_b_f[j]), LANES), gq, gk)
            y = _fox_attn(qt, ka, vt)
            wo, bo = c_w_out[j], zero_bias
        else:
            w_g2 = jnp.pad(d_w_g2[j], ((0, LANES - GLA_GATE_RANK), (0, 0)))
            y = _gla(x, g, d_w_in[j].astype(BF16), _pad_cols(d_w_g1[j], LANES).astype(BF16),
                     w_g2.astype(BF16), _row(d_b_g[j]), _row(d_o_norm[j]))
            wo, bo = d_w_out[j], zero_bias
        x = _proj_mlp(x.reshape(B * S, D_MODEL), y.reshape(B * S, D_MODEL), wo.astype(BF16), bo,
                      _row(mlp_norm[i]), mlp_w1[i].astype(BF16), mlp_w2[i].astype(BF16)).reshape(B, S, D_MODEL)
    return x
```

```python
import functools

import numpy as np
import jax
import jax.numpy as jnp
from jax import lax
from jax.experimental import pallas as pl
from jax.experimental.pallas import tpu as pltpu

F32 = jnp.float32
BF16 = jnp.bfloat16

D_MODEL = 1024
D_FF = 4 * D_MODEL
RMS_EPS = 1e-6
CONV_KERNEL = 31
SHORT_CONV = 3
FOX_HEADS = 16
FOX_HEAD_DIM = D_MODEL // FOX_HEADS
GLA_HEADS = 4
GLA_DK = D_MODEL // 2
GLA_DV = D_MODEL
GLA_DK_HEAD = GLA_DK // GLA_HEADS
GLA_DV_HEAD = GLA_DV // GLA_HEADS
GLA_GATE_RANK = 16
GLA_TAU = 16.0
GLA_CHUNK = 64

LANES = 128
SUBLANES = 8
CONV_ROWS = 64
CONV_COLS = 256
CONV_HALO = 32
SHORT_HALO = 8
NEG_BIG = -1e30
VMEM_LIMIT = 56 * 1024 * 1024

_NT = (((1,), (1,)), ((), ()))
_TN = (((0,), (0,)), ((), ()))


def _dot(a, b):
    return jnp.dot(a, b, preferred_element_type=F32)


def _rms(x, g):
    return x * lax.rsqrt(jnp.mean(x * x, axis=-1, keepdims=True) + RMS_EPS) * g


def _sigmoid(x):
    return 0.5 * jnp.tanh(0.5 * x) + 0.5


def _log_sigmoid(x):
    return jnp.minimum(x, 0.0) - jnp.log1p(jnp.exp(-jnp.abs(x)))


def _split3(x):
    h1 = x.astype(BF16)
    r1 = x - h1.astype(F32)
    h2 = r1.astype(BF16)
    h3 = (r1 - h2.astype(F32)).astype(BF16)
    return h1, h2, h3


def _const_spec(shape):
    nd = len(shape)
    return pl.BlockSpec(shape, lambda *_: (0,) * nd, pipeline_mode=pl.Buffered(1))


def _params(*sem):
    return pltpu.CompilerParams(dimension_semantics=sem, vmem_limit_bytes=VMEM_LIMIT)


MLP_TM = 512
MLP_TF = 1024


def _mlp_kernel(x_ref, y_ref, wo_ref, bo_ref, g_ref, w1_ref, w2_ref, o_ref):
    x1 = x_ref[...] + _dot(y_ref[...], wo_ref[...]) + bo_ref[...]
    hn = _rms(x1, g_ref[...]).astype(BF16)
    acc = x1
    for c in range(D_FF // MLP_TF):
        cols = slice(c * MLP_TF, (c + 1) * MLP_TF)
        h = jnp.maximum(_dot(hn, w1_ref[:, cols]), 0.0)
        acc = acc + _dot((h * h).astype(BF16), w2_ref[cols, :])
    o_ref[...] = acc


def _layer_spec(shape, layer):
    nd = len(shape)
    return pl.BlockSpec((None,) + tuple(shape), lambda *_: (layer,) + (0,) * nd, pipeline_mode=pl.Buffered(1))


def _proj_mlp(x2d, y2d, wo, bo, g, w1_all, w2_all, layer):
    T = x2d.shape[0]
    tm = min(MLP_TM, T)
    row = lambda i: (i, 0)
    return pl.pallas_call(
        _mlp_kernel,
        out_shape=jax.ShapeDtypeStruct((T, D_MODEL), F32),
        grid=(T // tm,),
        in_specs=[
            pl.BlockSpec((tm, D_MODEL), row),
            pl.BlockSpec((tm, D_MODEL), row),
            _const_spec((D_MODEL, D_MODEL)),
            _const_spec((1, D_MODEL)),
            _const_spec((1, D_MODEL)),
            _layer_spec((D_MODEL, D_FF), layer),
            _layer_spec((D_FF, D_MODEL), layer),
        ],
        out_specs=pl.BlockSpec((tm, D_MODEL), row),
        compiler_params=_params("parallel"),
        name="proj_mlp",
    )(x2d, y2d, wo, bo, g, w1_all, w2_all)


SEQ_TILE = 512


def _conformer_kernel(xh_ref, x_ref, g_ref, win_ref, bin_ref, cw_ref, cb_ref, lg_ref, lb_ref,
                      y_ref, glu_ref, sh_ref, *, ts):
    i = pl.program_id(1)
    xa = jnp.concatenate([xh_ref[...], x_ref[...]], axis=0)
    hn = _rms(xa, g_ref[...]).astype(BF16)
    p = _dot(hn, win_ref[...]) + bin_ref[...]
    glu = p[:, :D_MODEL] * _sigmoid(p[:, D_MODEL:])
    rows = lax.broadcasted_iota(jnp.int32, (CONV_HALO + ts, 1), 0)
    glu = jnp.where((rows >= CONV_HALO) | (i > 0), glu, 0.0)
    glu_ref[...] = glu
    span = ts + CONV_HALO - SUBLANES
    for r in range(1, SUBLANES):
        sh_ref[r - 1] = pltpu.roll(glu, CONV_HALO + ts - r, 0)[:span]
    first = CONV_HALO - (CONV_KERNEL - 1)
    taps = {}
    for k in range(CONV_KERNEL):
        a, r = divmod(first + k, SUBLANES)
        taps.setdefault(r, []).append((k, a * SUBLANES))

    def row_chunk(rc, carry):
        r0 = pl.multiple_of(rc * CONV_ROWS, CONV_ROWS)
        pieces = []
        for cb in range(D_MODEL // CONV_COLS):
            cols = slice(cb * CONV_COLS, (cb + 1) * CONV_COLS)
            acc = jnp.broadcast_to(cb_ref[:, cols], (CONV_ROWS, CONV_COLS))
            for r, group in taps.items():
                at = pl.ds(r0, CONV_ROWS + group[-1][1])
                win = glu_ref[at, cols] if r == 0 else sh_ref[r - 1, at, cols]
                for k, off in group:
                    acc = acc + cw_ref[k:k + 1, cols] * win[off:off + CONV_ROWS]
            pieces.append(acc)
        acc = jnp.concatenate(pieces, axis=1)
        xc = acc - jnp.mean(acc, axis=-1, keepdims=True)
        y = xc * lax.rsqrt(jnp.mean(xc * xc, axis=-1, keepdims=True) + RMS_EPS)
        y = y * lg_ref[...] + lb_ref[...]
        y_ref[pl.ds(r0, CONV_ROWS), :] = (y * _sigmoid(y)).astype(BF16)
        return carry

    lax.fori_loop(0, ts // CONV_ROWS, row_chunk, 0)


def _conformer(x, g, w_in, b_in, conv_w, conv_b, ln_g, ln_b):
    B, S, _ = x.shape
    ts = min(SEQ_TILE, S)
    per = ts // CONV_HALO
    kern = functools.partial(_conformer_kernel, ts=ts)
    return pl.pallas_call(
        kern,
        out_shape=jax.ShapeDtypeStruct((B, S, D_MODEL), BF16),
        grid=(B, S // ts),
        in_specs=[
            pl.BlockSpec((None, CONV_HALO, D_MODEL), lambda b, i: (b, jnp.maximum(i * per - 1, 0), 0)),
            pl.BlockSpec((None, ts, D_MODEL), lambda b, i: (b, i, 0)),
            _const_spec((1, D_MODEL)),
            _const_spec((D_MODEL, 2 * D_MODEL)),
            _const_spec((1, 2 * D_MODEL)),
            _const_spec((CONV_KERNEL, D_MODEL)),
            _const_spec((1, D_MODEL)),
            _const_spec((1, D_MODEL)),
            _const_spec((1, D_MODEL)),
        ],
        out_specs=pl.BlockSpec((None, ts, D_MODEL), lambda b, i: (b, i, 0)),
        scratch_shapes=[pltpu.VMEM((CONV_HALO + ts, D_MODEL), F32),
                        pltpu.VMEM((SUBLANES - 1, ts + CONV_HALO - SUBLANES, D_MODEL), F32)],
        compiler_params=_params("parallel", "parallel"),
        name="conformer",
    )(x, x, g, w_in, b_in, conv_w, conv_b, ln_g, ln_b)


def _short_conv_kernel(xh_ref, x_ref, g_ref, win_ref, cw_ref, y_ref, u_ref, *, ts):
    i = pl.program_id(1)
    xa = jnp.concatenate([xh_ref[...], x_ref[...]], axis=0)
    hn = _rms(xa, g_ref[...]).astype(BF16)
    p = _dot(hn, win_ref[...])
    u = p[:, D_MODEL:2 * D_MODEL] * p[:, 2 * D_MODEL:]
    rows = lax.broadcasted_iota(jnp.int32, (SHORT_HALO + ts, 1), 0)
    u_ref[...] = jnp.where((rows >= SHORT_HALO) | (i > 0), u, 0.0)
    first = SHORT_HALO - (SHORT_CONV - 1)
    conv = cw_ref[0:1, :] * u_ref[pl.ds(first, ts), :]
    for k in range(1, SHORT_CONV):
        conv = conv + cw_ref[k:k + 1, :] * u_ref[pl.ds(first + k, ts), :]
    y_ref[...] = (p[SHORT_HALO:, :D_MODEL] * conv).astype(BF16)


def _short_conv(x, g, w_in, conv_w):
    B, S, _ = x.shape
    ts = min(SEQ_TILE, S)
    per = ts // SHORT_HALO
    kern = functools.partial(_short_conv_kernel, ts=ts)
    return pl.pallas_call(
        kern,
        out_shape=jax.ShapeDtypeStruct((B, S, D_MODEL), BF16),
        grid=(B, S // ts),
        in_specs=[
            pl.BlockSpec((None, SHORT_HALO, D_MODEL), lambda b, i: (b, jnp.maximum(i * per - 1, 0), 0)),
            pl.BlockSpec((None, ts, D_MODEL), lambda b, i: (b, i, 0)),
            _const_spec((1, D_MODEL)),
            _const_spec((D_MODEL, 3 * D_MODEL)),
            _const_spec((SHORT_CONV, D_MODEL)),
        ],
        out_specs=pl.BlockSpec((None, ts, D_MODEL), lambda b, i: (b, i, 0)),
        scratch_shapes=[pltpu.VMEM((SHORT_HALO + ts, D_MODEL), F32)],
        compiler_params=_params("parallel", "parallel"),
        name="short_conv",
    )(x, x, g, w_in, conv_w)


FOX_SCALE = FOX_HEAD_DIM ** -0.5
LOG2E = 1.4426950408889634
N_SPLIT = 3
FOX_TQ = 1024
FOX_DEN_ROWS = 16
FOX_TQS = 256
FOX_TK = 256


def _fox_constants():
    H, dh = FOX_HEADS, FOX_HEAD_DIM
    p_q = np.zeros((LANES, D_MODEL), np.float32)
    p_k = np.zeros((LANES, D_MODEL), np.float32)
    one_q = np.zeros((1, D_MODEL), np.float32)
    one_k = np.zeros((1, D_MODEL), np.float32)
    for h in range(H):
        base = (h // 2) * LANES + (dh if h % 2 == 0 else 0)
        for s in range(N_SPLIT):
            p_q[s * H + h, base + s] = 1.0
            one_q[0, base + N_SPLIT + s] = 1.0
            one_k[0, base + s] = 1.0
            p_k[s * H + h, base + N_SPLIT + s] = -1.0
    return jnp.asarray(p_q, BF16), jnp.asarray(p_k, BF16), jnp.asarray(one_q), jnp.asarray(one_k)


def _pack_pieces(pieces):
    out = pieces[0].astype(F32)
    for s in range(1, N_SPLIT):
        out = out + pltpu.roll(pieces[s].astype(F32), s * FOX_HEADS, 1)
    return out.astype(BF16)


def _pair_rms(blk, gain, low):
    sq = blk * blk
    s_low = jnp.sum(jnp.where(low, sq, 0.0), axis=-1, keepdims=True)
    s_high = jnp.sum(jnp.where(low, 0.0, sq), axis=-1, keepdims=True)
    inv = jnp.where(low, lax.rsqrt(s_low * (1.0 / FOX_HEAD_DIM) + RMS_EPS),
                    lax.rsqrt(s_high * (1.0 / FOX_HEAD_DIM) + RMS_EPS))
    return blk * inv * gain


def _fox_proj_kernel(x_ref, g_ref, wqkv_ref, wf_ref, bf_ref, gq_ref, gk_ref,
                     pq_ref, pk_ref, oneq_ref, onek_ref, qt_ref, ka_ref, vt_ref, carry_ref, *, ts, tk):
    @pl.when(pl.program_id(1) == 0)
    def _():
        carry_ref[...] = jnp.zeros_like(carry_ref)

    hn = _rms(x_ref[...], g_ref[...]).astype(BF16)
    qkv = _dot(hn, wqkv_ref[...])

    lane = lax.broadcasted_iota(jnp.int32, (1, LANES), 1)
    logf = _log_sigmoid(_dot(hn, wf_ref[...]) + bf_ref[...])
    logf = jnp.where(lane < FOX_HEADS, logf, 0.0)
    tri = (lax.broadcasted_iota(jnp.int32, (ts, ts), 0)
           >= lax.broadcasted_iota(jnp.int32, (ts, ts), 1)).astype(BF16)
    cs = _dot(tri, _pack_pieces(_split3(logf)))
    c = cs + pltpu.roll(cs, LANES - FOX_HEADS, 1) + pltpu.roll(cs, LANES - 2 * FOX_HEADS, 1)
    c = jnp.where(lane < FOX_HEADS, c, 0.0) + carry_ref[...]
    carry_ref[...] = c[ts - 1:ts, :]
    packed = _pack_pieces(_split3(c * LOG2E))
    aug_q = _dot(packed, pq_ref[...]) + oneq_ref[...]
    aug_k = _dot(packed, pk_ref[...]) + onek_ref[...]

    low = lane < FOX_HEAD_DIM
    for p in range(FOX_HEADS // 2):
        blk = slice(p * LANES, (p + 1) * LANES)
        qn = _pair_rms(qkv[:, blk], gq_ref[:, blk], low)
        kn = _pair_rms(qkv[:, D_MODEL + p * LANES:D_MODEL + (p + 1) * LANES], gk_ref[:, blk], low)
        qt_ref[2 * p] = jnp.where(low, qn, aug_q[:, blk]).T.astype(BF16)
        qt_ref[2 * p + 1] = jnp.where(low, aug_q[:, blk], qn).T.astype(BF16)
        ka_ref[2 * p] = jnp.where(low, kn, aug_k[:, blk]).astype(BF16)
        ka_ref[2 * p + 1] = jnp.where(low, aug_k[:, blk], kn).astype(BF16)
        vt = qkv[:, 2 * D_MODEL + p * LANES:2 * D_MODEL + (p + 1) * LANES].T
        for jj in range(ts // tk):
            vt_ref[p, jj] = vt[:, jj * tk:(jj + 1) * tk].astype(BF16)


def _fox_proj(x, g, w_qkv, w_f, b_f, gq, gk):
    B, S, _ = x.shape
    ts = min(SEQ_TILE, S)
    tk = min(FOX_TK, S)
    consts = _fox_constants()
    kern = functools.partial(_fox_proj_kernel, ts=ts, tk=tk)
    tile = pl.BlockSpec((None, ts, D_MODEL), lambda b, i: (b, i, 0))
    return pl.pallas_call(
        kern,
        out_shape=(jax.ShapeDtypeStruct((B, FOX_HEADS, LANES, S), BF16),
                   jax.ShapeDtypeStruct((B, FOX_HEADS, S, LANES), BF16),
                   jax.ShapeDtypeStruct((B, FOX_HEADS // 2, S // tk, LANES, tk), BF16)),
        grid=(B, S // ts),
        in_specs=[tile, _const_spec((1, D_MODEL)), _const_spec((D_MODEL, 3 * D_MODEL)),
                  _const_spec((D_MODEL, LANES)), _const_spec((1, LANES)),
                  _const_spec((1, D_MODEL)), _const_spec((1, D_MODEL))]
                 + [_const_spec(c.shape) for c in consts],
        out_specs=(pl.BlockSpec((None, FOX_HEADS, LANES, ts), lambda b, i: (b, 0, 0, i)),
                   pl.BlockSpec((None, FOX_HEADS, ts, LANES), lambda b, i: (b, 0, i, 0)),
                   pl.BlockSpec((None, FOX_HEADS // 2, ts // tk, LANES, tk), lambda b, i: (b, 0, i, 0, 0))),
        scratch_shapes=[pltpu.VMEM((1, LANES), F32)],
        compiler_params=_params("parallel", "arbitrary"),
        name="fox_proj",
    )(x, g, w_qkv, w_f, b_f, gq, gk, *consts)


def _fox_attn_kernel(qt_ref, k_ref, vt_ref, o_ref, st_ref, *, tq, tqs, tk):
    i = pl.program_id(2)
    nsub = tq // tqs
    nkb = tq // tk
    dh = FOX_HEAD_DIM
    chains = [(h, qs) for h in range(2) for qs in range(nsub)]

    def scores(n, jblk):
        h, qs = chains[n]
        koff = pl.multiple_of(jblk * tk, tk)
        return _dot(k_ref[h, pl.ds(koff, tk), :], qt_ref[h, :, qs * tqs:(qs + 1) * tqs])

    def softmax_step(st, m):
        m_new = jnp.maximum(m, jnp.max(st, axis=0, keepdims=True))
        return m_new, jnp.exp2(m - m_new), jnp.exp2(st - m_new).astype(BF16)

    ones_rows = jnp.ones((FOX_DEN_ROWS, tk), BF16)

    def values(n, jblk, alpha, p, a):
        h, _ = chains[n]
        vt = jnp.concatenate([vt_ref[jblk, h * dh:(h + 1) * dh, :], ones_rows], axis=0)
        return alpha * a + _dot(vt, p)

    nch = len(chains)

    def block(jblk, slot, states, masks, next_masks):
        cur = slot * nch
        nxt = nch - cur
        out = list(states)

        def issue_next(n):
            if n < nch and next_masks is not None and next_masks[n] is not False:
                st_ref[nxt + n] = scores(n, jblk + 1)

        issue_next(0)
        for n in range(nch):
            issue_next(n + 1)
            if masks[n] is False:
                continue
            st = st_ref[cur + n]
            if masks[n] is not None:
                st = jnp.where(masks[n], st, NEG_BIG)
            m_new, alpha, p = softmax_step(st, states[n][0])
            out[n] = (m_new, values(n, jblk, alpha, p, states[n][1]))
        return tuple(out)

    row = lax.broadcasted_iota(jnp.int32, (tk, tqs), 0)
    col = lax.broadcasted_iota(jnp.int32, (tk, tqs), 1)
    diag_masks = []
    for kb in range(nkb):
        masks = []
        for _, qs in chains:
            k_lo, q_lo = kb * tk, qs * tqs
            if k_lo > q_lo + tqs - 1:
                masks.append(False)
            else:
                masks.append(None if k_lo + tk - 1 <= q_lo else (row + k_lo <= col + q_lo))
        diag_masks.append(masks)
    no_mask = [None] * len(chains)

    states = tuple((jnp.full((1, tqs), NEG_BIG, F32), jnp.zeros((dh + FOX_DEN_ROWS, tqs), F32))
                   for _ in chains)
    for n in range(nch):
        st_ref[n] = scores(n, 0)
    def two_blocks(jj, c):
        c = block(2 * jj, 0, c, no_mask, no_mask)
        return block(2 * jj + 1, 1, c, no_mask, no_mask)

    states = lax.fori_loop(0, i * (nkb // 2), two_blocks, states)
    for kb in range(nkb):
        states = block(i * nkb + kb, kb % 2, states, diag_masks[kb], diag_masks[kb + 1] if kb + 1 < nkb else None)
    def normalized(n):
        acc = states[n][1]
        return acc[:dh] / acc[dh:dh + 1]

    out_t = jnp.concatenate(
        [jnp.concatenate([normalized(h * nsub + qs) for qs in range(nsub)], axis=1) for h in range(2)],
        axis=0)
    o_ref[...] = out_t.T.astype(BF16)


def _fox_attn(qt, ka, vt):
    B, H, _, S = qt.shape
    tq, tqs, tk = min(FOX_TQ, S), min(FOX_TQS, S), min(FOX_TK, S)
    kern = functools.partial(_fox_attn_kernel, tq=tq, tqs=tqs, tk=tk)
    return pl.pallas_call(
        kern,
        out_shape=jax.ShapeDtypeStruct((B, S, D_MODEL), BF16),
        grid=(B, H // 2, S // tq),
        in_specs=[
            pl.BlockSpec((None, 2, LANES, tq), lambda b, p, i: (b, p, 0, i)),
            pl.BlockSpec((None, 2, S, LANES), lambda b, p, i: (b, p, 0, 0)),
            pl.BlockSpec((None, None, S // tk, LANES, tk), lambda b, p, i: (b, p, 0, 0, 0)),
        ],
        out_specs=pl.BlockSpec((None, tq, LANES), lambda b, p, i: (b, i, p)),
        scratch_shapes=[pltpu.VMEM((2 * 2 * (tq // tqs), tk, tqs), F32)],
        compiler_params=_params("parallel", "parallel", "arbitrary"),
        name="fox_attn",
    )(qt, ka, vt)


GLA_SCALE = GLA_DK_HEAD ** -0.5


def _gla_kernel(x_ref, g_ref, win_ref, wg1_ref, wg2_ref, bg_ref, og_ref, y_ref, st_ref, *, ts):
    @pl.when(pl.program_id(1) == 0)
    def _():
        st_ref[...] = jnp.zeros_like(st_ref)

    C = GLA_CHUNK
    hn = _rms(x_ref[...], g_ref[...]).astype(BF16)
    proj = _dot(hn, win_ref[...])
    gate = _dot(_dot(hn, wg1_ref[...]).astype(BF16), wg2_ref[...]) + bg_ref[...]
    log_a = _log_sigmoid(gate) / GLA_TAU
    tri = (lax.broadcasted_iota(jnp.int32, (C, C), 0)
           >= lax.broadcasted_iota(jnp.int32, (C, C), 1))
    tri3 = jnp.concatenate([tri.astype(BF16)] * N_SPLIT, axis=1)
    og = og_ref[...]
    for c in range(ts // C):
        rows = slice(c * C, (c + 1) * C)
        b = _dot(tri3, jnp.concatenate(_split3(log_a[rows]), axis=0))
        b_last = b[C - 1:C, :]
        q_t = proj[rows, :GLA_DK] * GLA_SCALE * jnp.exp(b)
        kc = proj[rows, GLA_DK:2 * GLA_DK]
        k_t = kc * jnp.exp(-b)
        k_dec = kc * jnp.exp(b_last - b)
        decay = jnp.exp(b_last)
        for h in range(GLA_HEADS):
            dk = slice(h * GLA_DK_HEAD, (h + 1) * GLA_DK_HEAD)
            dv = slice(h * GLA_DV_HEAD, (h + 1) * GLA_DV_HEAD)
            qh = q_t[:, dk].astype(BF16)
            vh = proj[rows, 2 * GLA_DK + h * GLA_DV_HEAD:2 * GLA_DK + (h + 1) * GLA_DV_HEAD].astype(BF16)
            att = lax.dot_general(qh, k_t[:, dk].astype(BF16), _NT, preferred_element_type=F32)
            att = jnp.where(tri, att, 0.0)
            st = st_ref[h]
            o = _dot(att.astype(BF16), vh) + lax.dot_general(qh, st.astype(BF16), _NT,
                                                             preferred_element_type=F32)
            st_ref[h] = st * decay[:, dk] + lax.dot_general(vh, k_dec[:, dk].astype(BF16), _TN,
                                                            preferred_element_type=F32)
            on = o * lax.rsqrt(jnp.mean(o * o, axis=-1, keepdims=True) + RMS_EPS) * og
            r = proj[rows, 2 * GLA_DK + GLA_DV + h * GLA_DV_HEAD:2 * GLA_DK + GLA_DV + (h + 1) * GLA_DV_HEAD]
            y_ref[rows, dv] = (on * (r * _sigmoid(r))).astype(BF16)


def _gla(x, g, w_in, w_g1, w_g2, b_g, o_g):
    B, S, _ = x.shape
    ts = min(SEQ_TILE, S)
    kern = functools.partial(_gla_kernel, ts=ts)
    tile = pl.BlockSpec((None, ts, D_MODEL), lambda b, i: (b, i, 0))
    return pl.pallas_call(
        kern,
        out_shape=jax.ShapeDtypeStruct((B, S, D_MODEL), BF16),
        grid=(B, S // ts),
        in_specs=[tile, _const_spec((1, D_MODEL)), _const_spec((D_MODEL, 2 * GLA_DK + 2 * GLA_DV)),
                  _const_spec((D_MODEL, LANES)), _const_spec((LANES, GLA_DK)),
                  _const_spec((1, GLA_DK)), _const_spec((1, GLA_DV_HEAD))],
        out_specs=tile,
        scratch_shapes=[pltpu.VMEM((GLA_HEADS, GLA_DV_HEAD, GLA_DK_HEAD), F32)],
        compiler_params=_params("parallel", "arbitrary"),
        name="gla",
    )(x, g, w_in, w_g1, w_g2, b_g, o_g)


def _row(v):
    return v.reshape(1, -1).astype(F32)


def _pad_cols(w, n):
    return jnp.pad(w, ((0, 0), (0, n - w.shape[1])))


def kernel(x, mix_norm, mlp_norm, mlp_w1, mlp_w2,
           a_w_in, a_b_in, a_conv_w, a_conv_b, a_ln_g, a_ln_b, a_w_out, a_b_out,
           b_w_in, b_conv_w, b_w_out,
           c_w_qkv, c_q_norm, c_k_norm, c_w_f, c_b_f, c_w_out,
           d_w_in, d_w_g1, d_w_g2, d_b_g, d_o_norm, d_w_out):
    B, S, _ = x.shape
    depth = mix_norm.shape[0]
    zero_bias = jnp.zeros((1, D_MODEL), F32)
    w1_all, w2_all = mlp_w1.astype(BF16), mlp_w2.astype(BF16)
    for i in range(depth):
        m, j = i % 4, i // 4
        g = _row(mix_norm[i])
        if m == 0:
            y = _conformer(x, g, a_w_in[j].astype(BF16), _row(a_b_in[j]), a_conv_w[j], _row(a_conv_b[j]),
                           _row(a_ln_g[j]), _row(a_ln_b[j]))
            wo, bo = a_w_out[j], _row(a_b_out[j])
        elif m == 1:
            y = _short_conv(x, g, b_w_in[j].astype(BF16), b_conv_w[j])
            wo, bo = b_w_out[j], zero_bias
        elif m == 2:
            gq = _row(jnp.tile(c_q_norm[j], FOX_HEADS)) * (FOX_SCALE * LOG2E)
            gk = _row(jnp.tile(c_k_norm[j], FOX_HEADS))
            qt, ka, vt = _fox_proj(x, g, c_w_qkv[j].astype(BF16), _pad_cols(c_w_f[j], LANES).astype(BF16),
                                  _pad_cols(_row(c_b_f[j]), LANES), gq, gk)
            y = _fox_attn(qt, ka, vt)
            wo, bo = c_w_out[j], zero_bias
        else:
            w_g2 = jnp.pad(d_w_g2[j], ((0, LANES - GLA_GATE_RANK), (0, 0)))
            y = _gla(x, g, d_w_in[j].astype(BF16), _pad_cols(d_w_g1[j], LANES).astype(BF16),
                     w_g2.astype(BF16), _row(d_b_g[j]), _row(d_o_norm[j]))
            wo, bo = d_w_out[j], zero_bias
        x = _proj_mlp(x.reshape(B * S, D_MODEL), y.reshape(B * S, D_MODEL), wo.astype(BF16), bo,
                      _row(mlp_norm[i]), w1_all, w2_all, i).reshape(B, S, D_MODEL)
    return x
```

```python
import functools

import numpy as np
import jax
import jax.numpy as jnp
from jax import lax
from jax.experimental import pallas as pl
from jax.experimental.pallas import tpu as pltpu

F32 = jnp.float32
BF16 = jnp.bfloat16

D_MODEL = 1024
D_FF = 4 * D_MODEL
RMS_EPS = 1e-6
CONV_KERNEL = 31
SHORT_CONV = 3
FOX_HEADS = 16
FOX_HEAD_DIM = D_MODEL // FOX_HEADS
GLA_HEADS = 4
GLA_DK = D_MODEL // 2
GLA_DV = D_MODEL
GLA_DK_HEAD = GLA_DK // GLA_HEADS
GLA_DV_HEAD = GLA_DV // GLA_HEADS
GLA_GATE_RANK = 16
GLA_TAU = 16.0
GLA_CHUNK = 64

LANES = 128
SUBLANES = 8
CONV_ROWS = 64
CONV_TAP_GROUP = 8
CONV_HALO = 32
SHORT_HALO = 8
NEG_BIG = -1e30
VMEM_LIMIT = 56 * 1024 * 1024

_NT = (((1,), (1,)), ((), ()))
_TN = (((0,), (0,)), ((), ()))


def _dot(a, b):
    return jnp.dot(a, b, preferred_element_type=F32)


def _rms(x, g):
    return x * lax.rsqrt(jnp.mean(x * x, axis=-1, keepdims=True) + RMS_EPS) * g


def _sigmoid(x):
    return 0.5 * jnp.tanh(0.5 * x) + 0.5


def _log_sigmoid(x):
    return jnp.minimum(x, 0.0) - jnp.log1p(jnp.exp(-jnp.abs(x)))


def _split3(x):
    h1 = x.astype(BF16)
    r1 = x - h1.astype(F32)
    h2 = r1.astype(BF16)
    h3 = (r1 - h2.astype(F32)).astype(BF16)
    return h1, h2, h3


def _const_spec(shape):
    nd = len(shape)
    return pl.BlockSpec(shape, lambda *_: (0,) * nd, pipeline_mode=pl.Buffered(1))


def _params(*sem):
    return pltpu.CompilerParams(dimension_semantics=sem, vmem_limit_bytes=VMEM_LIMIT)


MLP_TM = 512
MLP_TF = 1024


def _mlp_kernel(x_ref, y_ref, wo_ref, bo_ref, g_ref, w1_ref, w2_ref, o_ref):
    x1 = x_ref[...] + _dot(y_ref[...], wo_ref[...]) + bo_ref[...]
    hn = _rms(x1, g_ref[...]).astype(BF16)
    acc = x1
    for c in range(D_FF // MLP_TF):
        cols = slice(c * MLP_TF, (c + 1) * MLP_TF)
        h = jnp.maximum(_dot(hn, w1_ref[:, cols]), 0.0)
        acc = acc + _dot((h * h).astype(BF16), w2_ref[cols, :])
    o_ref[...] = acc


def _layer_spec(shape, layer):
    nd = len(shape)
    return pl.BlockSpec((None,) + tuple(shape), lambda *_: (layer,) + (0,) * nd, pipeline_mode=pl.Buffered(1))


def _proj_mlp(x2d, y2d, wo, bo, g, w1_all, w2_all, layer):
    T = x2d.shape[0]
    tm = min(MLP_TM, T)
    row = lambda i: (i, 0)
    return pl.pallas_call(
        _mlp_kernel,
        out_shape=jax.ShapeDtypeStruct((T, D_MODEL), F32),
        grid=(T // tm,),
        in_specs=[
            pl.BlockSpec((tm, D_MODEL), row),
            pl.BlockSpec((tm, D_MODEL), row),
            _const_spec((D_MODEL, D_MODEL)),
            _const_spec((1, D_MODEL)),
            _const_spec((1, D_MODEL)),
            _layer_spec((D_MODEL, D_FF), layer),
            _layer_spec((D_FF, D_MODEL), layer),
        ],
        out_specs=pl.BlockSpec((tm, D_MODEL), row),
        compiler_params=_params("parallel"),
        name="proj_mlp",
    )(x2d, y2d, wo, bo, g, w1_all, w2_all)


SEQ_TILE = 512


def _conformer_kernel(xh_ref, x_ref, g_ref, win_ref, bin_ref, cw_ref, cb_ref, lg_ref, lb_ref,
                      y_ref, glu_ref, out_ref, *, ts):
    i = pl.program_id(1)
    xa = jnp.concatenate([xh_ref[...], x_ref[...]], axis=0)
    hn = _rms(xa, g_ref[...]).astype(BF16)
    p = _dot(hn, win_ref[...]) + bin_ref[...]
    glu = p[:, :D_MODEL] * _sigmoid(p[:, D_MODEL:])
    rows = lax.broadcasted_iota(jnp.int32, (CONV_HALO + ts, 1), 0)
    glu = jnp.where((rows >= CONV_HALO) | (i > 0), glu, 0.0)
    nlb = D_MODEL // LANES
    for lb in range(nlb):
        glu_ref[lb] = glu[:, lb * LANES:(lb + 1) * LANES]
    first = CONV_HALO - (CONV_KERNEL - 1)

    def lane_block(lb, carry):
        bias = cb_ref[lb]
        w = [cw_ref[lb, k] for k in range(CONV_KERNEL)]

        def row_chunk(rc, carry):
            r0 = pl.multiple_of(rc * CONV_ROWS, CONV_ROWS)
            accs = [bias] * SUBLANES
            for s in range(CONV_KERNEL + SUBLANES - 1):
                rows_s = glu_ref[lb, pl.ds(r0 + first + s, SUBLANES, stride=SUBLANES), :]
                for m in range(SUBLANES):
                    if 0 <= s - m < CONV_KERNEL:
                        accs[m] = accs[m] + w[s - m] * rows_s
            for m in range(SUBLANES):
                out_ref[lb, pl.ds(r0 + m, SUBLANES, stride=SUBLANES), :] = accs[m]
            return carry

        return lax.fori_loop(0, ts // CONV_ROWS, row_chunk, carry, unroll=2)

    lax.fori_loop(0, nlb, lane_block, 0)
    acc = jnp.concatenate([out_ref[lb] for lb in range(nlb)], axis=1)
    xc = acc - jnp.mean(acc, axis=-1, keepdims=True)
    y = xc * lax.rsqrt(jnp.mean(xc * xc, axis=-1, keepdims=True) + RMS_EPS)
    y = y * lg_ref[...] + lb_ref[...]
    y_ref[...] = (y * _sigmoid(y)).astype(BF16)


def _conformer(x, g, w_in, b_in, conv_w, conv_b, ln_g, ln_b):
    B, S, _ = x.shape
    ts = min(SEQ_TILE, S)
    per = ts // CONV_HALO
    kern = functools.partial(_conformer_kernel, ts=ts)
    nlb = D_MODEL // LANES
    conv_w = jnp.broadcast_to(conv_w.reshape(CONV_KERNEL, nlb, 1, LANES).transpose(1, 0, 2, 3),
                              (nlb, CONV_KERNEL, SUBLANES, LANES))
    conv_b = jnp.broadcast_to(conv_b.reshape(nlb, 1, LANES), (nlb, SUBLANES, LANES))
    return pl.pallas_call(
        kern,
        out_shape=jax.ShapeDtypeStruct((B, S, D_MODEL), BF16),
        grid=(B, S // ts),
        in_specs=[
            pl.BlockSpec((None, CONV_HALO, D_MODEL), lambda b, i: (b, jnp.maximum(i * per - 1, 0), 0)),
            pl.BlockSpec((None, ts, D_MODEL), lambda b, i: (b, i, 0)),
            _const_spec((1, D_MODEL)),
            _const_spec((D_MODEL, 2 * D_MODEL)),
            _const_spec((1, 2 * D_MODEL)),
            _const_spec((nlb, CONV_KERNEL, SUBLANES, LANES)),
            _const_spec((nlb, SUBLANES, LANES)),
            _const_spec((1, D_MODEL)),
            _const_spec((1, D_MODEL)),
        ],
        out_specs=pl.BlockSpec((None, ts, D_MODEL), lambda b, i: (b, i, 0)),
        scratch_shapes=[pltpu.VMEM((D_MODEL // LANES, CONV_HALO + ts, LANES), F32),
                        pltpu.VMEM((D_MODEL // LANES, ts, LANES), F32)],
        compiler_params=_params("parallel", "parallel"),
        name="conformer",
    )(x, x, g, w_in, b_in, conv_w, conv_b, ln_g, ln_b)


def _short_conv_kernel(xh_ref, x_ref, g_ref, win_ref, cw_ref, y_ref, u_ref, *, ts):
    i = pl.program_id(1)
    xa = jnp.concatenate([xh_ref[...], x_ref[...]], axis=0)
    hn = _rms(xa, g_ref[...]).astype(BF16)
    p = _dot(hn, win_ref[...])
    u = p[:, D_MODEL:2 * D_MODEL] * p[:, 2 * D_MODEL:]
    rows = lax.broadcasted_iota(jnp.int32, (SHORT_HALO + ts, 1), 0)
    u_ref[...] = jnp.where((rows >= SHORT_HALO) | (i > 0), u, 0.0)
    first = SHORT_HALO - (SHORT_CONV - 1)
    conv = cw_ref[0:1, :] * u_ref[pl.ds(first, ts), :]
    for k in range(1, SHORT_CONV):
        conv = conv + cw_ref[k:k + 1, :] * u_ref[pl.ds(first + k, ts), :]
    y_ref[...] = (p[SHORT_HALO:, :D_MODEL] * conv).astype(BF16)


def _short_conv(x, g, w_in, conv_w):
    B, S, _ = x.shape
    ts = min(SEQ_TILE, S)
    per = ts // SHORT_HALO
    kern = functools.partial(_short_conv_kernel, ts=ts)
    return pl.pallas_call(
        kern,
        out_shape=jax.ShapeDtypeStruct((B, S, D_MODEL), BF16),
        grid=(B, S // ts),
        in_specs=[
            pl.BlockSpec((None, SHORT_HALO, D_MODEL), lambda b, i: (b, jnp.maximum(i * per - 1, 0), 0)),
            pl.BlockSpec((None, ts, D_MODEL), lambda b, i: (b, i, 0)),
            _const_spec((1, D_MODEL)),
            _const_spec((D_MODEL, 3 * D_MODEL)),
            _const_spec((SHORT_CONV, D_MODEL)),
        ],
        out_specs=pl.BlockSpec((None, ts, D_MODEL), lambda b, i: (b, i, 0)),
        scratch_shapes=[pltpu.VMEM((SHORT_HALO + ts, D_MODEL), F32)],
        compiler_params=_params("parallel", "parallel"),
        name="short_conv",
    )(x, x, g, w_in, conv_w)


FOX_SCALE = FOX_HEAD_DIM ** -0.5
LOG2E = 1.4426950408889634
N_SPLIT = 3
FOX_TQ = 1024
FOX_DEN_ROWS = 16
FOX_TQS = 256
FOX_TK = 256


def _fox_constants():
    H, dh = FOX_HEADS, FOX_HEAD_DIM
    p_q = np.zeros((LANES, D_MODEL), np.float32)
    p_k = np.zeros((LANES, D_MODEL), np.float32)
    one_q = np.zeros((1, D_MODEL), np.float32)
    one_k = np.zeros((1, D_MODEL), np.float32)
    for h in range(H):
        base = (h // 2) * LANES + (dh if h % 2 == 0 else 0)
        for s in range(N_SPLIT):
            p_q[s * H + h, base + s] = 1.0
            one_q[0, base + N_SPLIT + s] = 1.0
            one_k[0, base + s] = 1.0
            p_k[s * H + h, base + N_SPLIT + s] = -1.0
    return jnp.asarray(p_q, BF16), jnp.asarray(p_k, BF16), jnp.asarray(one_q), jnp.asarray(one_k)


def _pack_pieces(pieces):
    out = pieces[0].astype(F32)
    for s in range(1, N_SPLIT):
        out = out + pltpu.roll(pieces[s].astype(F32), s * FOX_HEADS, 1)
    return out.astype(BF16)


def _pair_rms(blk, gain, low):
    sq = blk * blk
    s_low = jnp.sum(jnp.where(low, sq, 0.0), axis=-1, keepdims=True)
    s_high = jnp.sum(jnp.where(low, 0.0, sq), axis=-1, keepdims=True)
    inv = jnp.where(low, lax.rsqrt(s_low * (1.0 / FOX_HEAD_DIM) + RMS_EPS),
                    lax.rsqrt(s_high * (1.0 / FOX_HEAD_DIM) + RMS_EPS))
    return blk * inv * gain


def _fox_proj_kernel(x_ref, g_ref, wqkv_ref, wf_ref, bf_ref, gq_ref, gk_ref,
                     pq_ref, pk_ref, oneq_ref, onek_ref, qt_ref, ka_ref, vt_ref, carry_ref, *, ts, tk):
    @pl.when(pl.program_id(1) == 0)
    def _():
        carry_ref[...] = jnp.zeros_like(carry_ref)

    hn = _rms(x_ref[...], g_ref[...]).astype(BF16)
    qkv = _dot(hn, wqkv_ref[...])

    lane = lax.broadcasted_iota(jnp.int32, (1, LANES), 1)
    logf = _log_sigmoid(_dot(hn, wf_ref[...]) + bf_ref[...])
    logf = jnp.where(lane < FOX_HEADS, logf, 0.0)
    tri = (lax.broadcasted_iota(jnp.int32, (ts, ts), 0)
           >= lax.broadcasted_iota(jnp.int32, (ts, ts), 1)).astype(BF16)
    cs = _dot(tri, _pack_pieces(_split3(logf)))
    c = cs + pltpu.roll(cs, LANES - FOX_HEADS, 1) + pltpu.roll(cs, LANES - 2 * FOX_HEADS, 1)
    c = jnp.where(lane < FOX_HEADS, c, 0.0) + carry_ref[...]
    carry_ref[...] = c[ts - 1:ts, :]
    packed = _pack_pieces(_split3(c * LOG2E))
    aug_q = _dot(packed, pq_ref[...]) + oneq_ref[...]
    aug_k = _dot(packed, pk_ref[...]) + onek_ref[...]

    low = lane < FOX_HEAD_DIM
    for p in range(FOX_HEADS // 2):
        blk = slice(p * LANES, (p + 1) * LANES)
        qn = _pair_rms(qkv[:, blk], gq_ref[:, blk], low)
        kn = _pair_rms(qkv[:, D_MODEL + p * LANES:D_MODEL + (p + 1) * LANES], gk_ref[:, blk], low)
        qt_ref[2 * p] = jnp.where(low, qn, aug_q[:, blk]).T.astype(BF16)
        qt_ref[2 * p + 1] = jnp.where(low, aug_q[:, blk], qn).T.astype(BF16)
        ka_ref[2 * p] = jnp.where(low, kn, aug_k[:, blk]).astype(BF16)
        ka_ref[2 * p + 1] = jnp.where(low, aug_k[:, blk], kn).astype(BF16)
        vt = qkv[:, 2 * D_MODEL + p * LANES:2 * D_MODEL + (p + 1) * LANES].T
        for jj in range(ts // tk):
            vt_ref[p, jj] = vt[:, jj * tk:(jj + 1) * tk].astype(BF16)


def _fox_proj(x, g, w_qkv, w_f, b_f, gq, gk):
    B, S, _ = x.shape
    ts = min(SEQ_TILE, S)
    tk = min(FOX_TK, S)
    consts = _fox_constants()
    kern = functools.partial(_fox_proj_kernel, ts=ts, tk=tk)
    tile = pl.BlockSpec((None, ts, D_MODEL), lambda b, i: (b, i, 0))
    return pl.pallas_call(
        kern,
        out_shape=(jax.ShapeDtypeStruct((B, FOX_HEADS, LANES, S), BF16),
                   jax.ShapeDtypeStruct((B, FOX_HEADS, S, LANES), BF16),
                   jax.ShapeDtypeStruct((B, FOX_HEADS // 2, S // tk, LANES, tk), BF16)),
        grid=(B, S // ts),
        in_specs=[tile, _const_spec((1, D_MODEL)), _const_spec((D_MODEL, 3 * D_MODEL)),
                  _const_spec((D_MODEL, LANES)), _const_spec((1, LANES)),
                  _const_spec((1, D_MODEL)), _const_spec((1, D_MODEL))]
                 + [_const_spec(c.shape) for c in consts],
        out_specs=(pl.BlockSpec((None, FOX_HEADS, LANES, ts), lambda b, i: (b, 0, 0, i)),
                   pl.BlockSpec((None, FOX_HEADS, ts, LANES), lambda b, i: (b, 0, i, 0)),
                   pl.BlockSpec((None, FOX_HEADS // 2, ts // tk, LANES, tk), lambda b, i: (b, 0, i, 0, 0))),
        scratch_shapes=[pltpu.VMEM((1, LANES), F32)],
        compiler_params=_params("parallel", "arbitrary"),
        name="fox_proj",
    )(x, g, w_qkv, w_f, b_f, gq, gk, *consts)


def _fox_attn_kernel(qt_ref, k_ref, vt_ref, o_ref, st_ref, *, tq, tqs, tk):
    i = pl.program_id(2)
    nsub = tq // tqs
    nkb = tq // tk
    dh = FOX_HEAD_DIM
    chains = [(h, qs) for h in range(2) for qs in range(nsub)]

    def scores(n, jblk):
        h, qs = chains[n]
        koff = pl.multiple_of(jblk * tk, tk)
        return _dot(k_ref[h, pl.ds(koff, tk), :], qt_ref[h, :, qs * tqs:(qs + 1) * tqs])

    def softmax_step(st, m):
        m_new = jnp.maximum(m, jnp.max(st, axis=0, keepdims=True))
        return m_new, jnp.exp2(m - m_new), jnp.exp2(st - m_new).astype(BF16)

    ones_rows = jnp.ones((FOX_DEN_ROWS, tk), BF16)

    def values(n, jblk, alpha, p, a):
        h, _ = chains[n]
        vt = jnp.concatenate([vt_ref[jblk, h * dh:(h + 1) * dh, :], ones_rows], axis=0)
        return alpha * a + _dot(vt, p)

    nch = len(chains)

    def block(jblk, slot, states, masks, next_masks):
        cur = slot * nch
        nxt = nch - cur
        out = list(states)

        def issue_next(n):
            if n < nch and next_masks is not None and next_masks[n] is not False:
                st_ref[nxt + n] = scores(n, jblk + 1)

        issue_next(0)
        for n in range(nch):
            issue_next(n + 1)
            if masks[n] is False:
                continue
            st = st_ref[cur + n]
            if masks[n] is not None:
                st = jnp.where(masks[n], st, NEG_BIG)
            m_new, alpha, p = softmax_step(st, states[n][0])
            out[n] = (m_new, values(n, jblk, alpha, p, states[n][1]))
        return tuple(out)

    row = lax.broadcasted_iota(jnp.int32, (tk, tqs), 0)
    col = lax.broadcasted_iota(jnp.int32, (tk, tqs), 1)
    diag_masks = []
    for kb in range(nkb):
        masks = []
        for _, qs in chains:
            k_lo, q_lo = kb * tk, qs * tqs
            if k_lo > q_lo + tqs - 1:
                masks.append(False)
            else:
                masks.append(None if k_lo + tk - 1 <= q_lo else (row + k_lo <= col + q_lo))
        diag_masks.append(masks)
    no_mask = [None] * len(chains)

    states = tuple((jnp.full((1, tqs), NEG_BIG, F32), jnp.zeros((dh + FOX_DEN_ROWS, tqs), F32))
                   for _ in chains)
    for n in range(nch):
        st_ref[n] = scores(n, 0)
    def two_blocks(jj, c):
        c = block(2 * jj, 0, c, no_mask, no_mask)
        return block(2 * jj + 1, 1, c, no_mask, no_mask)

    states = lax.fori_loop(0, i * (nkb // 2), two_blocks, states)
    for kb in range(nkb):
        states = block(i * nkb + kb, kb % 2, states, diag_masks[kb], diag_masks[kb + 1] if kb + 1 < nkb else None)
    def normalized(n):
        acc = states[n][1]
        return acc[:dh] / acc[dh:dh + 1]

    out_t = jnp.concatenate(
        [jnp.concatenate([normalized(h * nsub + qs) for qs in range(nsub)], axis=1) for h in range(2)],
        axis=0)
    o_ref[...] = out_t.T.astype(BF16)


def _fox_attn(qt, ka, vt):
    B, H, _, S = qt.shape
    tq, tqs, tk = min(FOX_TQ, S), min(FOX_TQS, S), min(FOX_TK, S)
    kern = functools.partial(_fox_attn_kernel, tq=tq, tqs=tqs, tk=tk)
    return pl.pallas_call(
        kern,
        out_shape=jax.ShapeDtypeStruct((B, S, D_MODEL), BF16),
        grid=(B, H // 2, S // tq),
        in_specs=[
            pl.BlockSpec((None, 2, LANES, tq), lambda b, p, i: (b, p, 0, i)),
            pl.BlockSpec((None, 2, S, LANES), lambda b, p, i: (b, p, 0, 0)),
            pl.BlockSpec((None, None, S // tk, LANES, tk), lambda b, p, i: (b, p, 0, 0, 0)),
        ],
        out_specs=pl.BlockSpec((None, tq, LANES), lambda b, p, i: (b, i, p)),
        scratch_shapes=[pltpu.VMEM((2 * 2 * (tq // tqs), tk, tqs), F32)],
        compiler_params=_params("parallel", "parallel", "arbitrary"),
        name="fox_attn",
    )(qt, ka, vt)


GLA_SCALE = GLA_DK_HEAD ** -0.5


def _gla_kernel(x_ref, g_ref, win_ref, wg1_ref, wg2_ref, bg_ref, og_ref, y_ref, st_ref, *, ts):
    @pl.when(pl.program_id(1) == 0)
    def _():
        st_ref[...] = jnp.zeros_like(st_ref)

    C = GLA_CHUNK
    hn = _rms(x_ref[...], g_ref[...]).astype(BF16)
    tri = (lax.broadcasted_iota(jnp.int32, (C, C), 0)
           >= lax.broadcasted_iota(jnp.int32, (C, C), 1))
    tri3 = jnp.concatenate([tri.astype(BF16)] * N_SPLIT, axis=1)
    og = og_ref[...]
    heads = range(GLA_HEADS)
    dks = [slice(h * GLA_DK_HEAD, (h + 1) * GLA_DK_HEAD) for h in heads]
    dvs = [slice(h * GLA_DV_HEAD, (h + 1) * GLA_DV_HEAD) for h in heads]
    proj = _dot(hn, win_ref[...])
    gate = _dot(_dot(hn, wg1_ref[...]).astype(BF16), wg2_ref[...]) + bg_ref[...]
    log_a = _log_sigmoid(gate) / GLA_TAU
    for c in range(ts // C):
        rows = slice(c * C, (c + 1) * C)
        b = _dot(tri3, jnp.concatenate(_split3(log_a[rows]), axis=0))
        b_last = b[C - 1:C, :]
        q_t = proj[rows, :GLA_DK] * GLA_SCALE * jnp.exp(b)
        kc = proj[rows, GLA_DK:2 * GLA_DK]
        k_t = kc * jnp.exp(-b)
        k_dec = kc * jnp.exp(b_last - b)
        decay = jnp.exp(b_last)
        qs = [q_t[:, dks[h]].astype(BF16) for h in heads]
        vs = [proj[rows, 2 * GLA_DK + h * GLA_DV_HEAD:2 * GLA_DK + (h + 1) * GLA_DV_HEAD].astype(BF16)
              for h in heads]
        atts = [lax.dot_general(qs[h], k_t[:, dks[h]].astype(BF16), _NT, preferred_element_type=F32)
                for h in heads]
        sts = [st_ref[h] for h in heads]
        inter = [lax.dot_general(qs[h], sts[h].astype(BF16), _NT, preferred_element_type=F32) for h in heads]
        for h in heads:
            st_ref[h] = sts[h] * decay[:, dks[h]] + lax.dot_general(vs[h], k_dec[:, dks[h]].astype(BF16), _TN,
                                                                    preferred_element_type=F32)
        for h in heads:
            o = _dot(jnp.where(tri, atts[h], 0.0).astype(BF16), vs[h]) + inter[h]
            on = o * lax.rsqrt(jnp.mean(o * o, axis=-1, keepdims=True) + RMS_EPS) * og
            r = proj[rows, 2 * GLA_DK + GLA_DV + h * GLA_DV_HEAD:2 * GLA_DK + GLA_DV + (h + 1) * GLA_DV_HEAD]
            y_ref[rows, dvs[h]] = (on * (r * _sigmoid(r))).astype(BF16)


def _gla(x, g, w_in, w_g1, w_g2, b_g, o_g):
    B, S, _ = x.shape
    ts = min(SEQ_TILE, S)
    kern = functools.partial(_gla_kernel, ts=ts)
    tile = pl.BlockSpec((None, ts, D_MODEL), lambda b, i: (b, i, 0))
    return pl.pallas_call(
        kern,
        out_shape=jax.ShapeDtypeStruct((B, S, D_MODEL), BF16),
        grid=(B, S // ts),
        in_specs=[tile, _const_spec((1, D_MODEL)), _const_spec((D_MODEL, 2 * GLA_DK + 2 * GLA_DV)),
                  _const_spec((D_MODEL, LANES)), _const_spec((LANES, GLA_DK)),
                  _const_spec((1, GLA_DK)), _const_spec((1, GLA_DV_HEAD))],
        out_specs=tile,
        scratch_shapes=[pltpu.VMEM((GLA_HEADS, GLA_DV_HEAD, GLA_DK_HEAD), F32)],
        compiler_params=_params("parallel", "arbitrary"),
        name="gla",
    )(x, g, w_in, w_g1, w_g2, b_g, o_g)


def _row(v):
    return v.reshape(1, -1).astype(F32)


def _pad_cols(w, n):
    return jnp.pad(w, ((0, 0), (0, n - w.shape[1])))


def kernel(x, mix_norm, mlp_norm, mlp_w1, mlp_w2,
           a_w_in, a_b_in, a_conv_w, a_conv_b, a_ln_g, a_ln_b, a_w_out, a_b_out,
           b_w_in, b_conv_w, b_w_out,
           c_w_qkv, c_q_norm, c_k_norm, c_w_f, c_b_f, c_w_out,
           d_w_in, d_w_g1, d_w_g2, d_b_g, d_o_norm, d_w_out):
    B, S, _ = x.shape
    depth = mix_norm.shape[0]
    zero_bias = jnp.zeros((1, D_MODEL), F32)
    w1_all, w2_all = mlp_w1.astype(BF16), mlp_w2.astype(BF16)
    for i in range(depth):
        m, j = i % 4, i // 4
        g = _row(mix_norm[i])
        if m == 0:
            y = _conformer(x, g, a_w_in[j].astype(BF16), _row(a_b_in[j]), a_conv_w[j], _row(a_conv_b[j]),
                           _row(a_ln_g[j]), _row(a_ln_b[j]))
            wo, bo = a_w_out[j], _row(a_b_out[j])
        elif m == 1:
            y = _short_conv(x, g, b_w_in[j].astype(BF16), b_conv_w[j])
            wo, bo = b_w_out[j], zero_bias
        elif m == 2:
            gq = _row(jnp.tile(c_q_norm[j], FOX_HEADS)) * (FOX_SCALE * LOG2E)
            gk = _row(jnp.tile(c_k_norm[j], FOX_HEADS))
            qt, ka, vt = _fox_proj(x, g, c_w_qkv[j].astype(BF16), _pad_cols(c_w_f[j], LANES).astype(BF16),
                                  _pad_cols(_row(c_b_f[j]), LANES), gq, gk)
            y = _fox_attn(qt, ka, vt)
            wo, bo = c_w_out[j], zero_bias
        else:
            w_g2 = jnp.pad(d_w_g2[j], ((0, LANES - GLA_GATE_RANK), (0, 0)))
            y = _gla(x, g, d_w_in[j].astype(BF16), _pad_cols(d_w_g1[j], LANES).astype(BF16),
                     w_g2.astype(BF16), _row(d_b_g[j]), _row(d_o_norm[j]))
            wo, bo = d_w_out[j], zero_bias
        x = _proj_mlp(x.reshape(B * S, D_MODEL), y.reshape(B * S, D_MODEL), wo.astype(BF16), bo,
                      _row(mlp_norm[i]), w1_all, w2_all, i).reshape(B, S, D_MODEL)
    return x
```

```python
import functools

import numpy as np
import jax
import jax.numpy as jnp
from jax import lax
from jax.experimental import pallas as pl
from jax.experimental.pallas import tpu as pltpu

F32 = jnp.float32
BF16 = jnp.bfloat16

D_MODEL = 1024
D_FF = 4 * D_MODEL
RMS_EPS = 1e-6
CONV_KERNEL = 31
SHORT_CONV = 3
FOX_HEADS = 16
FOX_HEAD_DIM = D_MODEL // FOX_HEADS
GLA_HEADS = 4
GLA_DK = D_MODEL // 2
GLA_DV = D_MODEL
GLA_DK_HEAD = GLA_DK // GLA_HEADS
GLA_DV_HEAD = GLA_DV // GLA_HEADS
GLA_GATE_RANK = 16
GLA_TAU = 16.0
GLA_CHUNK = 64

LANES = 128
SUBLANES = 8
CONV_ROWS = 64
CONV_TAP_GROUP = 8
CONV_HALO = 32
SHORT_HALO = 8
NEG_BIG = -1e30
VMEM_LIMIT = 56 * 1024 * 1024

_NT = (((1,), (1,)), ((), ()))
_TN = (((0,), (0,)), ((), ()))


def _dot(a, b):
    return jnp.dot(a, b, preferred_element_type=F32)


def _rms(x, g):
    return x * lax.rsqrt(jnp.mean(x * x, axis=-1, keepdims=True) + RMS_EPS) * g


def _sigmoid(x):
    return 0.5 * jnp.tanh(0.5 * x) + 0.5


def _log_sigmoid(x):
    return jnp.minimum(x, 0.0) - jnp.log1p(jnp.exp(-jnp.abs(x)))


def _split3(x):
    h1 = x.astype(BF16)
    r1 = x - h1.astype(F32)
    h2 = r1.astype(BF16)
    h3 = (r1 - h2.astype(F32)).astype(BF16)
    return h1, h2, h3


def _const_spec(shape):
    nd = len(shape)
    return pl.BlockSpec(shape, lambda *_: (0,) * nd, pipeline_mode=pl.Buffered(1))


def _params(*sem):
    return pltpu.CompilerParams(dimension_semantics=sem, vmem_limit_bytes=VMEM_LIMIT)


MLP_TM = 512
MLP_TF = 1024


def _mlp_kernel(x_ref, y_ref, wo_ref, bo_ref, g_ref, w1_ref, w2_ref, o_ref):
    x1 = x_ref[...] + _dot(y_ref[...], wo_ref[...]) + bo_ref[...]
    hn = _rms(x1, g_ref[...]).astype(BF16)
    acc = x1
    for c in range(D_FF // MLP_TF):
        cols = slice(c * MLP_TF, (c + 1) * MLP_TF)
        h = jnp.maximum(_dot(hn, w1_ref[:, cols]), 0.0)
        acc = acc + _dot((h * h).astype(BF16), w2_ref[cols, :])
    o_ref[...] = acc


def _layer_spec(shape, layer):
    nd = len(shape)
    return pl.BlockSpec((None,) + tuple(shape), lambda *_: (layer,) + (0,) * nd, pipeline_mode=pl.Buffered(1))


def _proj_mlp(x2d, y2d, wo, bo, g, w1_all, w2_all, layer):
    T = x2d.shape[0]
    tm = min(MLP_TM, T)
    row = lambda i: (i, 0)
    return pl.pallas_call(
        _mlp_kernel,
        out_shape=jax.ShapeDtypeStruct((T, D_MODEL), F32),
        grid=(T // tm,),
        in_specs=[
            pl.BlockSpec((tm, D_MODEL), row),
            pl.BlockSpec((tm, D_MODEL), row),
            _const_spec((D_MODEL, D_MODEL)),
            _const_spec((1, D_MODEL)),
            _const_spec((1, D_MODEL)),
            _layer_spec((D_MODEL, D_FF), layer),
            _layer_spec((D_FF, D_MODEL), layer),
        ],
        out_specs=pl.BlockSpec((tm, D_MODEL), row),
        compiler_params=_params("parallel"),
        name="proj_mlp",
    )(x2d, y2d, wo, bo, g, w1_all, w2_all)


SEQ_TILE = 512


def _conformer_kernel(xh_ref, x_ref, g_ref, win_ref, bin_ref, cw_ref, cb_ref, lg_ref, lb_ref,
                      y_ref, glu_ref, out_ref, *, ts):
    i = pl.program_id(1)
    xa = jnp.concatenate([xh_ref[...], x_ref[...]], axis=0)
    hn = _rms(xa, g_ref[...]).astype(BF16)
    p = _dot(hn, win_ref[...]) + bin_ref[...]
    glu = p[:, :D_MODEL] * _sigmoid(p[:, D_MODEL:])
    rows = lax.broadcasted_iota(jnp.int32, (CONV_HALO + ts, 1), 0)
    glu = jnp.where((rows >= CONV_HALO) | (i > 0), glu, 0.0)
    nlb = D_MODEL // LANES
    for lb in range(nlb):
        glu_ref[lb] = glu[:, lb * LANES:(lb + 1) * LANES]
    first = CONV_HALO - (CONV_KERNEL - 1)

    def lane_block(lb, carry):
        bias = cb_ref[lb]
        w = [cw_ref[lb, k] for k in range(CONV_KERNEL)]

        def row_chunk(rc, carry):
            r0 = pl.multiple_of(rc * CONV_ROWS, CONV_ROWS)
            accs = [bias] * SUBLANES
            for s in range(CONV_KERNEL + SUBLANES - 1):
                rows_s = glu_ref[lb, pl.ds(r0 + first + s, SUBLANES, stride=SUBLANES), :]
                for m in range(SUBLANES):
                    if 0 <= s - m < CONV_KERNEL:
                        accs[m] = accs[m] + w[s - m] * rows_s
            for m in range(SUBLANES):
                out_ref[lb, pl.ds(r0 + m, SUBLANES, stride=SUBLANES), :] = accs[m]
            return carry

        return lax.fori_loop(0, ts // CONV_ROWS, row_chunk, carry, unroll=2)

    lax.fori_loop(0, nlb, lane_block, 0)
    acc = jnp.concatenate([out_ref[lb] for lb in range(nlb)], axis=1)
    xc = acc - jnp.mean(acc, axis=-1, keepdims=True)
    y = xc * lax.rsqrt(jnp.mean(xc * xc, axis=-1, keepdims=True) + RMS_EPS)
    y = y * lg_ref[...] + lb_ref[...]
    y_ref[...] = (y * _sigmoid(y)).astype(BF16)


def _conformer(x, g, w_in, b_in, conv_w, conv_b, ln_g, ln_b):
    B, S, _ = x.shape
    ts = min(SEQ_TILE, S)
    per = ts // CONV_HALO
    kern = functools.partial(_conformer_kernel, ts=ts)
    nlb = D_MODEL // LANES
    conv_w = jnp.broadcast_to(conv_w.reshape(CONV_KERNEL, nlb, 1, LANES).transpose(1, 0, 2, 3),
                              (nlb, CONV_KERNEL, SUBLANES, LANES))
    conv_b = jnp.broadcast_to(conv_b.reshape(nlb, 1, LANES), (nlb, SUBLANES, LANES))
    return pl.pallas_call(
        kern,
        out_shape=jax.ShapeDtypeStruct((B, S, D_MODEL), BF16),
        grid=(B, S // ts),
        in_specs=[
            pl.BlockSpec((None, CONV_HALO, D_MODEL), lambda b, i: (b, jnp.maximum(i * per - 1, 0), 0)),
            pl.BlockSpec((None, ts, D_MODEL), lambda b, i: (b, i, 0)),
            _const_spec((1, D_MODEL)),
            _const_spec((D_MODEL, 2 * D_MODEL)),
            _const_spec((1, 2 * D_MODEL)),
            _const_spec((nlb, CONV_KERNEL, SUBLANES, LANES)),
            _const_spec((nlb, SUBLANES, LANES)),
            _const_spec((1, D_MODEL)),
            _const_spec((1, D_MODEL)),
        ],
        out_specs=pl.BlockSpec((None, ts, D_MODEL), lambda b, i: (b, i, 0)),
        scratch_shapes=[pltpu.VMEM((D_MODEL // LANES, CONV_HALO + ts, LANES), F32),
                        pltpu.VMEM((D_MODEL // LANES, ts, LANES), F32)],
        compiler_params=_params("parallel", "parallel"),
        name="conformer",
    )(x, x, g, w_in, b_in, conv_w, conv_b, ln_g, ln_b)


def _short_conv_kernel(xh_ref, x_ref, g_ref, win_ref, cw_ref, y_ref, u_ref, *, ts):
    i = pl.program_id(1)
    xa = jnp.concatenate([xh_ref[...], x_ref[...]], axis=0)
    hn = _rms(xa, g_ref[...]).astype(BF16)
    p = _dot(hn, win_ref[...])
    u = p[:, D_MODEL:2 * D_MODEL] * p[:, 2 * D_MODEL:]
    rows = lax.broadcasted_iota(jnp.int32, (SHORT_HALO + ts, 1), 0)
    u_ref[...] = jnp.where((rows >= SHORT_HALO) | (i > 0), u, 0.0)
    first = SHORT_HALO - (SHORT_CONV - 1)
    conv = cw_ref[0:1, :] * u_ref[pl.ds(first, ts), :]
    for k in range(1, SHORT_CONV):
        conv = conv + cw_ref[k:k + 1, :] * u_ref[pl.ds(first + k, ts), :]
    y_ref[...] = (p[SHORT_HALO:, :D_MODEL] * conv).astype(BF16)


def _short_conv(x, g, w_in, conv_w):
    B, S, _ = x.shape
    ts = min(SEQ_TILE, S)
    per = ts // SHORT_HALO
    kern = functools.partial(_short_conv_kernel, ts=ts)
    return pl.pallas_call(
        kern,
        out_shape=jax.ShapeDtypeStruct((B, S, D_MODEL), BF16),
        grid=(B, S // ts),
        in_specs=[
            pl.BlockSpec((None, SHORT_HALO, D_MODEL), lambda b, i: (b, jnp.maximum(i * per - 1, 0), 0)),
            pl.BlockSpec((None, ts, D_MODEL), lambda b, i: (b, i, 0)),
            _const_spec((1, D_MODEL)),
            _const_spec((D_MODEL, 3 * D_MODEL)),
            _const_spec((SHORT_CONV, D_MODEL)),
        ],
        out_specs=pl.BlockSpec((None, ts, D_MODEL), lambda b, i: (b, i, 0)),
        scratch_shapes=[pltpu.VMEM((SHORT_HALO + ts, D_MODEL), F32)],
        compiler_params=_params("parallel", "parallel"),
        name="short_conv",
    )(x, x, g, w_in, conv_w)


FOX_SCALE = FOX_HEAD_DIM ** -0.5
LOG2E = 1.4426950408889634
N_SPLIT = 3
FOX_TQ = 1024
FOX_BLOCKS_PER_TRIP = 4
FOX_DEN_ROWS = 16
FOX_TQS = 256
FOX_TK = 256


def _fox_constants():
    H, dh = FOX_HEADS, FOX_HEAD_DIM
    p_q = np.zeros((LANES, D_MODEL), np.float32)
    p_k = np.zeros((LANES, D_MODEL), np.float32)
    one_q = np.zeros((1, D_MODEL), np.float32)
    one_k = np.zeros((1, D_MODEL), np.float32)
    for h in range(H):
        base = (h // 2) * LANES + (dh if h % 2 == 0 else 0)
        for s in range(N_SPLIT):
            p_q[s * H + h, base + s] = 1.0
            one_q[0, base + N_SPLIT + s] = 1.0
            one_k[0, base + s] = 1.0
            p_k[s * H + h, base + N_SPLIT + s] = -1.0
    return jnp.asarray(p_q, BF16), jnp.asarray(p_k, BF16), jnp.asarray(one_q), jnp.asarray(one_k)


def _pack_pieces(pieces):
    out = pieces[0].astype(F32)
    for s in range(1, N_SPLIT):
        out = out + pltpu.roll(pieces[s].astype(F32), s * FOX_HEADS, 1)
    return out.astype(BF16)


def _pair_rms(blk, gain, low):
    sq = blk * blk
    s_low = jnp.sum(jnp.where(low, sq, 0.0), axis=-1, keepdims=True)
    s_high = jnp.sum(jnp.where(low, 0.0, sq), axis=-1, keepdims=True)
    inv = jnp.where(low, lax.rsqrt(s_low * (1.0 / FOX_HEAD_DIM) + RMS_EPS),
                    lax.rsqrt(s_high * (1.0 / FOX_HEAD_DIM) + RMS_EPS))
    return blk * inv * gain


def _fox_proj_kernel(x_ref, g_ref, wqkv_ref, wf_ref, bf_ref, gq_ref, gk_ref,
                     pq_ref, pk_ref, oneq_ref, onek_ref, qt_ref, ka_ref, vt_ref, carry_ref, *, ts, tk):
    @pl.when(pl.program_id(1) == 0)
    def _():
        carry_ref[...] = jnp.zeros_like(carry_ref)

    hn = _rms(x_ref[...], g_ref[...]).astype(BF16)
    qkv = _dot(hn, wqkv_ref[...])

    lane = lax.broadcasted_iota(jnp.int32, (1, LANES), 1)
    logf = _log_sigmoid(_dot(hn, wf_ref[...]) + bf_ref[...])
    logf = jnp.where(lane < FOX_HEADS, logf, 0.0)
    tri = (lax.broadcasted_iota(jnp.int32, (ts, ts), 0)
           >= lax.broadcasted_iota(jnp.int32, (ts, ts), 1)).astype(BF16)
    cs = _dot(tri, _pack_pieces(_split3(logf)))
    c = cs + pltpu.roll(cs, LANES - FOX_HEADS, 1) + pltpu.roll(cs, LANES - 2 * FOX_HEADS, 1)
    c = jnp.where(lane < FOX_HEADS, c, 0.0) + carry_ref[...]
    carry_ref[...] = c[ts - 1:ts, :]
    packed = _pack_pieces(_split3(c * LOG2E))
    aug_q = _dot(packed, pq_ref[...]) + oneq_ref[...]
    aug_k = _dot(packed, pk_ref[...]) + onek_ref[...]

    low = lane < FOX_HEAD_DIM
    for p in range(FOX_HEADS // 2):
        blk = slice(p * LANES, (p + 1) * LANES)
        qn = _pair_rms(qkv[:, blk], gq_ref[:, blk], low)
        kn = _pair_rms(qkv[:, D_MODEL + p * LANES:D_MODEL + (p + 1) * LANES], gk_ref[:, blk], low)
        qt_ref[2 * p] = jnp.where(low, qn, aug_q[:, blk]).T.astype(BF16)
        qt_ref[2 * p + 1] = jnp.where(low, aug_q[:, blk], qn).T.astype(BF16)
        ka_ref[2 * p] = jnp.where(low, kn, aug_k[:, blk]).astype(BF16)
        ka_ref[2 * p + 1] = jnp.where(low, aug_k[:, blk], kn).astype(BF16)
        vt = qkv[:, 2 * D_MODEL + p * LANES:2 * D_MODEL + (p + 1) * LANES].T
        for jj in range(ts // tk):
            vt_ref[p, jj] = vt[:, jj * tk:(jj + 1) * tk].astype(BF16)


def _fox_proj(x, g, w_qkv, w_f, b_f, gq, gk):
    B, S, _ = x.shape
    ts = min(SEQ_TILE, S)
    tk = min(FOX_TK, S)
    consts = _fox_constants()
    kern = functools.partial(_fox_proj_kernel, ts=ts, tk=tk)
    tile = pl.BlockSpec((None, ts, D_MODEL), lambda b, i: (b, i, 0))
    return pl.pallas_call(
        kern,
        out_shape=(jax.ShapeDtypeStruct((B, FOX_HEADS, LANES, S), BF16),
                   jax.ShapeDtypeStruct((B, FOX_HEADS, S, LANES), BF16),
                   jax.ShapeDtypeStruct((B, FOX_HEADS // 2, S // tk, LANES, tk), BF16)),
        grid=(B, S // ts),
        in_specs=[tile, _const_spec((1, D_MODEL)), _const_spec((D_MODEL, 3 * D_MODEL)),
                  _const_spec((D_MODEL, LANES)), _const_spec((1, LANES)),
                  _const_spec((1, D_MODEL)), _const_spec((1, D_MODEL))]
                 + [_const_spec(c.shape) for c in consts],
        out_specs=(pl.BlockSpec((None, FOX_HEADS, LANES, ts), lambda b, i: (b, 0, 0, i)),
                   pl.BlockSpec((None, FOX_HEADS, ts, LANES), lambda b, i: (b, 0, i, 0)),
                   pl.BlockSpec((None, FOX_HEADS // 2, ts // tk, LANES, tk), lambda b, i: (b, 0, i, 0, 0))),
        scratch_shapes=[pltpu.VMEM((1, LANES), F32)],
        compiler_params=_params("parallel", "arbitrary"),
        name="fox_proj",
    )(x, g, w_qkv, w_f, b_f, gq, gk, *consts)


def _fox_attn_kernel(qt_ref, qt_next_ref, k_ref, vt_ref, o_ref, st_ref, *, tq, tqs, tk):
    i = pl.program_id(2)
    nsub = tq // tqs
    nkb = tq // tk
    dh = FOX_HEAD_DIM
    chains = [(h, qs) for h in range(2) for qs in range(nsub)]

    def scores(n, jblk, q_ref=qt_ref):
        h, qs = chains[n]
        koff = pl.multiple_of(jblk * tk, tk)
        return _dot(k_ref[h, pl.ds(koff, tk), :], q_ref[h, :, qs * tqs:(qs + 1) * tqs])

    def softmax_step(st, m):
        m_new = jnp.maximum(m, jnp.max(st, axis=0, keepdims=True))
        return m_new, jnp.exp2(m - m_new), jnp.exp2(st - m_new).astype(BF16)

    ones_rows = jnp.ones((FOX_DEN_ROWS, tk), BF16)

    def values(n, jblk, alpha, p, a):
        h, _ = chains[n]
        vt = jnp.concatenate([vt_ref[jblk, h * dh:(h + 1) * dh, :], ones_rows], axis=0)
        return alpha * a + _dot(vt, p)

    nch = len(chains)

    def block(jblk, slot, states, masks, next_masks):
        cur = slot * nch
        nxt = nch - cur
        out = list(states)

        def issue_next(n):
            if n >= nch:
                return
            if next_masks is None:
                st_ref[nxt + n] = scores(n, 0, qt_next_ref)
            elif next_masks[n] is not False:
                st_ref[nxt + n] = scores(n, jblk + 1)

        issue_next(0)
        for n in range(nch):
            issue_next(n + 1)
            if masks[n] is False:
                continue
            st = st_ref[cur + n]
            if masks[n] is not None:
                st = jnp.where(masks[n], st, NEG_BIG)
            m_new, alpha, p = softmax_step(st, states[n][0])
            out[n] = (m_new, values(n, jblk, alpha, p, states[n][1]))
        return tuple(out)

    row = lax.broadcasted_iota(jnp.int32, (tk, tqs), 0)
    col = lax.broadcasted_iota(jnp.int32, (tk, tqs), 1)
    diag_masks = []
    for kb in range(nkb):
        masks = []
        for _, qs in chains:
            k_lo, q_lo = kb * tk, qs * tqs
            if k_lo > q_lo + tqs - 1:
                masks.append(False)
            else:
                masks.append(None if k_lo + tk - 1 <= q_lo else (row + k_lo <= col + q_lo))
        diag_masks.append(masks)
    no_mask = [None] * len(chains)

    states = tuple((jnp.full((1, tqs), NEG_BIG, F32), jnp.zeros((dh + FOX_DEN_ROWS, tqs), F32))
                   for _ in chains)
    @pl.when(i == 0)
    def _():
        for n in range(nch):
            st_ref[n] = scores(n, 0)

    def one_trip(jj, c):
        for u in range(FOX_BLOCKS_PER_TRIP):
            c = block(FOX_BLOCKS_PER_TRIP * jj + u, u % 2, c, no_mask, no_mask)
        return c

    states = lax.fori_loop(0, i * (nkb // FOX_BLOCKS_PER_TRIP), one_trip, states)
    for kb in range(nkb):
        states = block(i * nkb + kb, kb % 2, states, diag_masks[kb], diag_masks[kb + 1] if kb + 1 < nkb else None)
    def normalized(n):
        acc = states[n][1]
        return acc[:dh] / acc[dh:dh + 1]

    out_t = jnp.concatenate(
        [jnp.concatenate([normalized(h * nsub + qs) for qs in range(nsub)], axis=1) for h in range(2)],
        axis=0)
    o_ref[...] = out_t.T.astype(BF16)


def _fox_attn(qt, ka, vt):
    B, H, _, S = qt.shape
    tq, tqs, tk = min(FOX_TQ, S), min(FOX_TQS, S), min(FOX_TK, S)
    kern = functools.partial(_fox_attn_kernel, tq=tq, tqs=tqs, tk=tk)
    assert (tq // tk) % FOX_BLOCKS_PER_TRIP == 0 or S == tq
    last = S // tq - 1
    return pl.pallas_call(
        kern,
        out_shape=jax.ShapeDtypeStruct((B, S, D_MODEL), BF16),
        grid=(B, H // 2, S // tq),
        in_specs=[
            pl.BlockSpec((None, 2, LANES, tq), lambda b, p, i: (b, p, 0, i)),
            pl.BlockSpec((None, 2, LANES, tq), lambda b, p, i: (b, p, 0, jnp.minimum(i + 1, last))),
            pl.BlockSpec((None, 2, S, LANES), lambda b, p, i: (b, p, 0, 0)),
            pl.BlockSpec((None, None, S // tk, LANES, tk), lambda b, p, i: (b, p, 0, 0, 0)),
        ],
        out_specs=pl.BlockSpec((None, tq, LANES), lambda b, p, i: (b, i, p)),
        scratch_shapes=[pltpu.VMEM((2 * 2 * (tq // tqs), tk, tqs), F32)],
        compiler_params=_params("parallel", "parallel", "arbitrary"),
        name="fox_attn",
    )(qt, qt, ka, vt)


GLA_SCALE = GLA_DK_HEAD ** -0.5


def _gla_kernel(x_ref, g_ref, win_ref, wg1_ref, wg2_ref, bg_ref, og_ref, y_ref, st_ref, *, ts):
    @pl.when(pl.program_id(1) == 0)
    def _():
        st_ref[...] = jnp.zeros_like(st_ref)

    C = GLA_CHUNK
    hn = _rms(x_ref[...], g_ref[...]).astype(BF16)
    tri = (lax.broadcasted_iota(jnp.int32, (C, C), 0)
           >= lax.broadcasted_iota(jnp.int32, (C, C), 1))
    tri3 = jnp.concatenate([tri.astype(BF16)] * N_SPLIT, axis=1)
    og = og_ref[...]
    heads = range(GLA_HEADS)
    dks = [slice(h * GLA_DK_HEAD, (h + 1) * GLA_DK_HEAD) for h in heads]
    dvs = [slice(h * GLA_DV_HEAD, (h + 1) * GLA_DV_HEAD) for h in heads]
    proj = _dot(hn, win_ref[...])
    gate = _dot(_dot(hn, wg1_ref[...]).astype(BF16), wg2_ref[...]) + bg_ref[...]
    log_a = _log_sigmoid(gate) / GLA_TAU
    for c in range(ts // C):
        rows = slice(c * C, (c + 1) * C)
        b = _dot(tri3, jnp.concatenate(_split3(log_a[rows]), axis=0))
        b_last = b[C - 1:C, :]
        q_t = proj[rows, :GLA_DK] * GLA_SCALE * jnp.exp(b)
        kc = proj[rows, GLA_DK:2 * GLA_DK]
        k_t = kc * jnp.exp(-b)
        k_dec = kc * jnp.exp(b_last - b)
        decay = jnp.exp(b_last)
        qs = [q_t[:, dks[h]].astype(BF16) for h in heads]
        vs = [proj[rows, 2 * GLA_DK + h * GLA_DV_HEAD:2 * GLA_DK + (h + 1) * GLA_DV_HEAD].astype(BF16)
              for h in heads]
        atts = [lax.dot_general(qs[h], k_t[:, dks[h]].astype(BF16), _NT, preferred_element_type=F32)
                for h in heads]
        sts = [st_ref[h] for h in heads]
        inter = [lax.dot_general(qs[h], sts[h].astype(BF16), _NT, preferred_element_type=F32) for h in heads]
        for h in heads:
            st_ref[h] = sts[h] * decay[:, dks[h]] + lax.dot_general(vs[h], k_dec[:, dks[h]].astype(BF16), _TN,
                                                                    preferred_element_type=F32)
        for h in heads:
            o = _dot(jnp.where(tri, atts[h], 0.0).astype(BF16), vs[h]) + inter[h]
            on = o * lax.rsqrt(jnp.mean(o * o, axis=-1, keepdims=True) + RMS_EPS) * og
            r = proj[rows, 2 * GLA_DK + GLA_DV + h * GLA_DV_HEAD:2 * GLA_DK + GLA_DV + (h + 1) * GLA_DV_HEAD]
            y_ref[rows, dvs[h]] = (on * (r * _sigmoid(r))).astype(BF16)


def _gla(x, g, w_in, w_g1, w_g2, b_g, o_g):
    B, S, _ = x.shape
    ts = min(SEQ_TILE, S)
    kern = functools.partial(_gla_kernel, ts=ts)
    tile = pl.BlockSpec((None, ts, D_MODEL), lambda b, i: (b, i, 0))
    return pl.pallas_call(
        kern,
        out_shape=jax.ShapeDtypeStruct((B, S, D_MODEL), BF16),
        grid=(B, S // ts),
        in_specs=[tile, _const_spec((1, D_MODEL)), _const_spec((D_MODEL, 2 * GLA_DK + 2 * GLA_DV)),
                  _const_spec((D_MODEL, LANES)), _const_spec((LANES, GLA_DK)),
                  _const_spec((1, GLA_DK)), _const_spec((1, GLA_DV_HEAD))],
        out_specs=tile,
        scratch_shapes=[pltpu.VMEM((GLA_HEADS, GLA_DV_HEAD, GLA_DK_HEAD), F32)],
        compiler_params=_params("parallel", "arbitrary"),
        name="gla",
    )(x, g, w_in, w_g1, w_g2, b_g, o_g)


def _row(v):
    return v.reshape(1, -1).astype(F32)


def _pad_cols(w, n):
    return jnp.pad(w, ((0, 0), (0, n - w.shape[1])))


def kernel(x, mix_norm, mlp_norm, mlp_w1, mlp_w2,
           a_w_in, a_b_in, a_conv_w, a_conv_b, a_ln_g, a_ln_b, a_w_out, a_b_out,
           b_w_in, b_conv_w, b_w_out,
           c_w_qkv, c_q_norm, c_k_norm, c_w_f, c_b_f, c_w_out,
           d_w_in, d_w_g1, d_w_g2, d_b_g, d_o_norm, d_w_out):
    B, S, _ = x.shape
    depth = mix_norm.shape[0]
    zero_bias = jnp.zeros((1, D_MODEL), F32)
    w1_all, w2_all = mlp_w1.astype(BF16), mlp_w2.astype(BF16)
    for i in range(depth):
        m, j = i % 4, i // 4
        g = _row(mix_norm[i])
        if m == 0:
            y = _conformer(x, g, a_w_in[j].astype(BF16), _row(a_b_in[j]), a_conv_w[j], _row(a_conv_b[j]),
                           _row(a_ln_g[j]), _row(a_ln_b[j]))
            wo, bo = a_w_out[j], _row(a_b_out[j])
        elif m == 1:
            y = _short_conv(x, g, b_w_in[j].astype(BF16), b_conv_w[j])
            wo, bo = b_w_out[j], zero_bias
        elif m == 2:
            gq = _row(jnp.tile(c_q_norm[j], FOX_HEADS)) * (FOX_SCALE * LOG2E)
            gk = _row(jnp.tile(c_k_norm[j], FOX_HEADS))
            qt, ka, vt = _fox_proj(x, g, c_w_qkv[j].astype(BF16), _pad_cols(c_w_f[j], LANES).astype(BF16),
                                  _pad_cols(_row(c_b_f[j]), LANES), gq, gk)
            y = _fox_attn(qt, ka, vt)
            wo, bo = c_w_out[j], zero_bias
        else:
            w_g2 = jnp.pad(d_w_g2[j], ((0, LANES - GLA_GATE_RANK), (0, 0)))
            y = _gla(x, g, d_w_in[j].astype(BF16), _pad_cols(d_w_g1[j], LANES).astype(BF16),
                     w_g2.astype(BF16), _row(d_b_g[j]), _row(d_o_norm[j]))
            wo, bo = d_w_out[j], zero_bias
        x = _proj_mlp(x.reshape(B * S, D_MODEL), y.reshape(B * S, D_MODEL), wo.astype(BF16), bo,
                      _row(mlp_norm[i]), w1_all, w2_all, i).reshape(B, S, D_MODEL)
    return x
```

```python
import functools

import numpy as np
import jax
import jax.numpy as jnp
from jax import lax
from jax.experimental import pallas as pl
from jax.experimental.pallas import tpu as pltpu

F32 = jnp.float32
BF16 = jnp.bfloat16

D_MODEL = 1024
D_FF = 4 * D_MODEL
RMS_EPS = 1e-6
CONV_KERNEL = 31
SHORT_CONV = 3
FOX_HEADS = 16
FOX_HEAD_DIM = D_MODEL // FOX_HEADS
GLA_HEADS = 4
GLA_DK = D_MODEL // 2
GLA_DV = D_MODEL
GLA_DK_HEAD = GLA_DK // GLA_HEADS
GLA_DV_HEAD = GLA_DV // GLA_HEADS
GLA_GATE_RANK = 16
GLA_TAU = 16.0
GLA_CHUNK = 64

LANES = 128
SUBLANES = 8
MXU_COLS = 256
CONV_ROWS = 64
CONV_TAP_GROUP = 8
CONV_HALO = 32
SHORT_HALO = 8
NEG_BIG = -1e30
VMEM_LIMIT = 56 * 1024 * 1024

_NT = (((1,), (1,)), ((), ()))
_TN = (((0,), (0,)), ((), ()))


def _dot(a, b):
    return jnp.dot(a, b, preferred_element_type=F32)


def _rms(x, g):
    return x * lax.rsqrt(jnp.mean(x * x, axis=-1, keepdims=True) + RMS_EPS) * g


def _sigmoid(x):
    return 0.5 * jnp.tanh(0.5 * x) + 0.5


def _log_sigmoid(x):
    return jnp.minimum(x, 0.0) - jnp.log1p(jnp.exp(-jnp.abs(x)))


def _split3(x):
    h1 = x.astype(BF16)
    r1 = x - h1.astype(F32)
    h2 = r1.astype(BF16)
    h3 = (r1 - h2.astype(F32)).astype(BF16)
    return h1, h2, h3


def _const_spec(shape):
    nd = len(shape)
    return pl.BlockSpec(shape, lambda *_: (0,) * nd, pipeline_mode=pl.Buffered(1))


def _params(*sem):
    return pltpu.CompilerParams(dimension_semantics=sem, vmem_limit_bytes=VMEM_LIMIT)


MLP_TM = 512
MLP_TF = 1024


def _mlp_kernel(x_ref, y_ref, wo_ref, bo_ref, g_ref, w1_ref, w2_ref, o_ref):
    x1 = x_ref[...] + _dot(y_ref[...], wo_ref[...]) + bo_ref[...]
    hn = _rms(x1, g_ref[...]).astype(BF16)
    acc = x1
    for c in range(D_FF // MLP_TF):
        cols = slice(c * MLP_TF, (c + 1) * MLP_TF)
        h = jnp.maximum(_dot(hn, w1_ref[:, cols]), 0.0)
        acc = acc + _dot((h * h).astype(BF16), w2_ref[cols, :])
    o_ref[...] = acc


def _layer_spec(shape, layer):
    nd = len(shape)
    return pl.BlockSpec((None,) + tuple(shape), lambda *_: (layer,) + (0,) * nd, pipeline_mode=pl.Buffered(1))


def _proj_mlp(x2d, y2d, wo, bo, g, w1_all, w2_all, layer):
    T = x2d.shape[0]
    tm = min(MLP_TM, T)
    row = lambda i: (i, 0)
    return pl.pallas_call(
        _mlp_kernel,
        out_shape=jax.ShapeDtypeStruct((T, D_MODEL), F32),
        grid=(T // tm,),
        in_specs=[
            pl.BlockSpec((tm, D_MODEL), row),
            pl.BlockSpec((tm, D_MODEL), row),
            _const_spec((D_MODEL, D_MODEL)),
            _const_spec((1, D_MODEL)),
            _const_spec((1, D_MODEL)),
            _layer_spec((D_MODEL, D_FF), layer),
            _layer_spec((D_FF, D_MODEL), layer),
        ],
        out_specs=pl.BlockSpec((tm, D_MODEL), row),
        compiler_params=_params("parallel"),
        name="proj_mlp",
    )(x2d, y2d, wo, bo, g, w1_all, w2_all)


SEQ_TILE = 512


def _conformer_kernel(xh_ref, x_ref, xn_ref, g_ref, w3_ref, b3_ref, cw_ref, cb_ref, lg_ref, lb_ref,
                      y_ref, glu_ref, out_ref, p_ref, hn_ref, *, ts):
    i = pl.program_id(1)
    nlb = D_MODEL // LANES
    npb = 2 * D_MODEL // MXU_COLS
    sub = MXU_COLS // LANES

    @pl.when(i == 0)
    def _():
        hn0 = _rms(jnp.concatenate([xh_ref[...], x_ref[...]], axis=0), g_ref[...]).astype(BF16)
        for pb in range(npb):
            p_ref[pb] = _dot(hn0, w3_ref[pb]) + b3_ref[pb]

    rows = lax.broadcasted_iota(jnp.int32, (CONV_HALO + ts, 1), 0)
    keep = (rows >= CONV_HALO) | (i > 0)
    for j in range(npb // 2):
        glu = jnp.where(keep, p_ref[j] * _sigmoid(p_ref[npb // 2 + j]), 0.0)
        for q in range(sub):
            glu_ref[j * sub + q] = glu[:, q * LANES:(q + 1) * LANES]
    for pb in range(npb):
        p_ref[pb, 0:CONV_HALO, :] = p_ref[pb, ts:ts + CONV_HALO, :]
    hn_ref[...] = _rms(xn_ref[...], g_ref[...]).astype(BF16)
    first = CONV_HALO - (CONV_KERNEL - 1)
    chunks_per_trip = 4
    dot_rows = ts // (ts // CONV_ROWS // chunks_per_trip)

    def lane_block(lb, carry):
        bias = cb_ref[lb]

        def row_chunk(r0):
            accs = [bias] * SUBLANES
            for s in range(CONV_KERNEL + SUBLANES - 1):
                rows_s = glu_ref[lb, pl.ds(r0 + first + s, SUBLANES, stride=SUBLANES), :]
                for m in range(SUBLANES):
                    if 0 <= s - m < CONV_KERNEL:
                        accs[m] = accs[m] + cw_ref[lb, s - m] * rows_s
            for m in range(SUBLANES):
                out_ref[lb, pl.ds(r0 + m, SUBLANES, stride=SUBLANES), :] = accs[m]

        def trip(t, carry):
            n_pieces = 2
            piece = dot_rows // n_pieces
            for c in range(chunks_per_trip):
                if c % (chunks_per_trip // n_pieces) == 0:
                    pr = pl.multiple_of(t * dot_rows + (c // (chunks_per_trip // n_pieces)) * piece, piece)
                    p_ref[lb, pl.ds(CONV_HALO + pr, piece), :] = (
                        _dot(hn_ref[pl.ds(pr, piece), :], w3_ref[lb]) + b3_ref[lb])
                row_chunk(pl.multiple_of((t * chunks_per_trip + c) * CONV_ROWS, CONV_ROWS))
            return carry

        return lax.fori_loop(0, ts // CONV_ROWS // chunks_per_trip, trip, carry)

    lax.fori_loop(0, nlb, lane_block, 0)
    acc = jnp.concatenate([out_ref[lb] for lb in range(nlb)], axis=1)
    xc = acc - jnp.mean(acc, axis=-1, keepdims=True)
    y = xc * lax.rsqrt(jnp.mean(xc * xc, axis=-1, keepdims=True) + RMS_EPS)
    y = y * lg_ref[...] + lb_ref[...]
    y_ref[...] = (y * _sigmoid(y)).astype(BF16)


def _conformer(x, g, w_in, b_in, conv_w, conv_b, ln_g, ln_b):
    B, S, _ = x.shape
    ts = min(SEQ_TILE, S)
    per = ts // CONV_HALO
    kern = functools.partial(_conformer_kernel, ts=ts)
    nlb = D_MODEL // LANES
    conv_w = jnp.broadcast_to(conv_w.reshape(CONV_KERNEL, nlb, 1, LANES).transpose(1, 0, 2, 3),
                              (nlb, CONV_KERNEL, SUBLANES, LANES))
    conv_b = jnp.broadcast_to(conv_b.reshape(nlb, 1, LANES), (nlb, SUBLANES, LANES))
    npb = 2 * D_MODEL // MXU_COLS
    assert npb == nlb
    w3 = w_in.reshape(D_MODEL, npb, MXU_COLS).transpose(1, 0, 2)
    b3 = b_in.reshape(npb, 1, MXU_COLS)
    last = S // ts - 1
    return pl.pallas_call(
        kern,
        out_shape=jax.ShapeDtypeStruct((B, S, D_MODEL), BF16),
        grid=(B, S // ts),
        in_specs=[
            pl.BlockSpec((None, CONV_HALO, D_MODEL), lambda b, i: (b, jnp.maximum(i * per - 1, 0), 0)),
            pl.BlockSpec((None, ts, D_MODEL), lambda b, i: (b, i, 0)),
            pl.BlockSpec((None, ts, D_MODEL), lambda b, i: (b, jnp.minimum(i + 1, last), 0)),
            _const_spec((1, D_MODEL)),
            _const_spec((npb, D_MODEL, MXU_COLS)),
            _const_spec((npb, 1, MXU_COLS)),
            _const_spec((nlb, CONV_KERNEL, SUBLANES, LANES)),
            _const_spec((nlb, SUBLANES, LANES)),
            _const_spec((1, D_MODEL)),
            _const_spec((1, D_MODEL)),
        ],
        out_specs=pl.BlockSpec((None, ts, D_MODEL), lambda b, i: (b, i, 0)),
        scratch_shapes=[pltpu.VMEM((nlb, CONV_HALO + ts, LANES), F32),
                        pltpu.VMEM((nlb, ts, LANES), F32),
                        pltpu.VMEM((npb, CONV_HALO + ts, MXU_COLS), F32),
                        pltpu.VMEM((ts, D_MODEL), BF16)],
        compiler_params=_params("parallel", "arbitrary"),
        name="conformer",
    )(x, x, x, g, w3, b3, conv_w, conv_b, ln_g, ln_b)


def _short_conv_kernel(xh_ref, x_ref, g_ref, win_ref, cw_ref, y_ref, u_ref, *, ts):
    i = pl.program_id(1)
    xa = jnp.concatenate([xh_ref[...], x_ref[...]], axis=0)
    hn = _rms(xa, g_ref[...]).astype(BF16)
    p = _dot(hn, win_ref[...])
    u = p[:, D_MODEL:2 * D_MODEL] * p[:, 2 * D_MODEL:]
    rows = lax.broadcasted_iota(jnp.int32, (SHORT_HALO + ts, 1), 0)
    u_ref[...] = jnp.where((rows >= SHORT_HALO) | (i > 0), u, 0.0)
    first = SHORT_HALO - (SHORT_CONV - 1)
    conv = cw_ref[0:1, :] * u_ref[pl.ds(first, ts), :]
    for k in range(1, SHORT_CONV):
        conv = conv + cw_ref[k:k + 1, :] * u_ref[pl.ds(first + k, ts), :]
    y_ref[...] = (p[SHORT_HALO:, :D_MODEL] * conv).astype(BF16)


def _short_conv(x, g, w_in, conv_w):
    B, S, _ = x.shape
    ts = min(SEQ_TILE, S)
    per = ts // SHORT_HALO
    kern = functools.partial(_short_conv_kernel, ts=ts)
    return pl.pallas_call(
        kern,
        out_shape=jax.ShapeDtypeStruct((B, S, D_MODEL), BF16),
        grid=(B, S // ts),
        in_specs=[
            pl.BlockSpec((None, SHORT_HALO, D_MODEL), lambda b, i: (b, jnp.maximum(i * per - 1, 0), 0)),
            pl.BlockSpec((None, ts, D_MODEL), lambda b, i: (b, i, 0)),
            _const_spec((1, D_MODEL)),
            _const_spec((D_MODEL, 3 * D_MODEL)),
            _const_spec((SHORT_CONV, D_MODEL)),
        ],
        out_specs=pl.BlockSpec((None, ts, D_MODEL), lambda b, i: (b, i, 0)),
        scratch_shapes=[pltpu.VMEM((SHORT_HALO + ts, D_MODEL), F32)],
        compiler_params=_params("parallel", "parallel"),
        name="short_conv",
    )(x, x, g, w_in, conv_w)


FOX_SCALE = FOX_HEAD_DIM ** -0.5
LOG2E = 1.4426950408889634
N_SPLIT = 3
FOX_TQ = 1024
FOX_BLOCKS_PER_TRIP = 4
FOX_DEN_ROWS = 16
FOX_TQS = 256
FOX_TK = 256


def _fox_constants():
    H, dh = FOX_HEADS, FOX_HEAD_DIM
    p_q = np.zeros((LANES, D_MODEL), np.float32)
    p_k = np.zeros((LANES, D_MODEL), np.float32)
    one_q = np.zeros((1, D_MODEL), np.float32)
    one_k = np.zeros((1, D_MODEL), np.float32)
    for h in range(H):
        base = (h // 2) * LANES + (dh if h % 2 == 0 else 0)
        for s in range(N_SPLIT):
            p_q[s * H + h, base + s] = 1.0
            one_q[0, base + N_SPLIT + s] = 1.0
            one_k[0, base + s] = 1.0
            p_k[s * H + h, base + N_SPLIT + s] = -1.0
    return jnp.asarray(p_q, BF16), jnp.asarray(p_k, BF16), jnp.asarray(one_q), jnp.asarray(one_k)


def _pack_pieces(pieces):
    out = pieces[0].astype(F32)
    for s in range(1, N_SPLIT):
        out = out + pltpu.roll(pieces[s].astype(F32), s * FOX_HEADS, 1)
    return out.astype(BF16)


def _pair_rms(blk, gain, low):
    sq = blk * blk
    s_low = jnp.sum(jnp.where(low, sq, 0.0), axis=-1, keepdims=True)
    s_high = jnp.sum(jnp.where(low, 0.0, sq), axis=-1, keepdims=True)
    inv = jnp.where(low, lax.rsqrt(s_low * (1.0 / FOX_HEAD_DIM) + RMS_EPS),
                    lax.rsqrt(s_high * (1.0 / FOX_HEAD_DIM) + RMS_EPS))
    return blk * inv * gain


def _fox_proj_kernel(x_ref, g_ref, wqkv_ref, wf_ref, bf_ref, gq_ref, gk_ref,
                     pq_ref, pk_ref, oneq_ref, onek_ref, qt_ref, ka_ref, vt_ref, carry_ref, *, ts, tk):
    @pl.when(pl.program_id(1) == 0)
    def _():
        carry_ref[...] = jnp.zeros_like(carry_ref)

    hn = _rms(x_ref[...], g_ref[...]).astype(BF16)
    qkv = _dot(hn, wqkv_ref[...])

    lane = lax.broadcasted_iota(jnp.int32, (1, LANES), 1)
    logf = _log_sigmoid(_dot(hn, wf_ref[...]) + bf_ref[...])
    logf = jnp.where(lane < FOX_HEADS, logf, 0.0)
    tri = (lax.broadcasted_iota(jnp.int32, (ts, ts), 0)
           >= lax.broadcasted_iota(jnp.int32, (ts, ts), 1)).astype(BF16)
    cs = _dot(tri, _pack_pieces(_split3(logf)))
    c = cs + pltpu.roll(cs, LANES - FOX_HEADS, 1) + pltpu.roll(cs, LANES - 2 * FOX_HEADS, 1)
    c = jnp.where(lane < FOX_HEADS, c, 0.0) + carry_ref[...]
    carry_ref[...] = c[ts - 1:ts, :]
    packed = _pack_pieces(_split3(c * LOG2E))
    aug_q = _dot(packed, pq_ref[...]) + oneq_ref[...]
    aug_k = _dot(packed, pk_ref[...]) + onek_ref[...]

    low = lane < FOX_HEAD_DIM
    for p in range(FOX_HEADS // 2):
        blk = slice(p * LANES, (p + 1) * LANES)
        qn = _pair_rms(qkv[:, blk], gq_ref[:, blk], low)
        kn = _pair_rms(qkv[:, D_MODEL + p * LANES:D_MODEL + (p + 1) * LANES], gk_ref[:, blk], low)
        qt_ref[2 * p] = jnp.where(low, qn, aug_q[:, blk]).T.astype(BF16)
        qt_ref[2 * p + 1] = jnp.where(low, aug_q[:, blk], qn).T.astype(BF16)
        ka_ref[2 * p] = jnp.where(low, kn, aug_k[:, blk]).astype(BF16)
        ka_ref[2 * p + 1] = jnp.where(low, aug_k[:, blk], kn).astype(BF16)
        vt = qkv[:, 2 * D_MODEL + p * LANES:2 * D_MODEL + (p + 1) * LANES].T
        for jj in range(ts // tk):
            vt_ref[p, jj] = vt[:, jj * tk:(jj + 1) * tk].astype(BF16)


def _fox_proj(x, g, w_qkv, w_f, b_f, gq, gk):
    B, S, _ = x.shape
    ts = min(SEQ_TILE, S)
    tk = min(FOX_TK, S)
    consts = _fox_constants()
    kern = functools.partial(_fox_proj_kernel, ts=ts, tk=tk)
    tile = pl.BlockSpec((None, ts, D_MODEL), lambda b, i: (b, i, 0))
    return pl.pallas_call(
        kern,
        out_shape=(jax.ShapeDtypeStruct((B, FOX_HEADS, LANES, S), BF16),
                   jax.ShapeDtypeStruct((B, FOX_HEADS, S, LANES), BF16),
                   jax.ShapeDtypeStruct((B, FOX_HEADS // 2, S // tk, LANES, tk), BF16)),
        grid=(B, S // ts),
        in_specs=[tile, _const_spec((1, D_MODEL)), _const_spec((D_MODEL, 3 * D_MODEL)),
                  _const_spec((D_MODEL, LANES)), _const_spec((1, LANES)),
                  _const_spec((1, D_MODEL)), _const_spec((1, D_MODEL))]
                 + [_const_spec(c.shape) for c in consts],
        out_specs=(pl.BlockSpec((None, FOX_HEADS, LANES, ts), lambda b, i: (b, 0, 0, i)),
                   pl.BlockSpec((None, FOX_HEADS, ts, LANES), lambda b, i: (b, 0, i, 0)),
                   pl.BlockSpec((None, FOX_HEADS // 2, ts // tk, LANES, tk), lambda b, i: (b, 0, i, 0, 0))),
        scratch_shapes=[pltpu.VMEM((1, LANES), F32)],
        compiler_params=_params("parallel", "arbitrary"),
        name="fox_proj",
    )(x, g, w_qkv, w_f, b_f, gq, gk, *consts)


def _fox_attn_kernel(qt_ref, qt_next_ref, k_ref, vt_ref, o_ref, st_ref, *, tq, tqs, tk):
    i = pl.program_id(2)
    nsub = tq // tqs
    nkb = tq // tk
    dh = FOX_HEAD_DIM
    chains = [(h, qs) for h in range(2) for qs in range(nsub)]

    def scores(n, jblk, q_ref=qt_ref):
        h, qs = chains[n]
        koff = pl.multiple_of(jblk * tk, tk)
        return _dot(k_ref[h, pl.ds(koff, tk), :], q_ref[h, :, qs * tqs:(qs + 1) * tqs])

    def softmax_step(st, m):
        m_new = jnp.maximum(m, jnp.max(st, axis=0, keepdims=True))
        return m_new, jnp.exp2(m - m_new), jnp.exp2(st - m_new).astype(BF16)

    ones_rows = jnp.ones((FOX_DEN_ROWS, tk), BF16)

    def values(n, jblk, alpha, p, a):
        h, _ = chains[n]
        vt = jnp.concatenate([vt_ref[jblk, h * dh:(h + 1) * dh, :], ones_rows], axis=0)
        return alpha * a + _dot(vt, p)

    nch = len(chains)

    def block(jblk, slot, states, masks, next_masks):
        cur = slot * nch
        nxt = nch - cur
        out = list(states)

        def issue_next(n):
            if n >= nch:
                return
            if next_masks is None:
                st_ref[nxt + n] = scores(n, 0, qt_next_ref)
            elif next_masks[n] is not False:
                st_ref[nxt + n] = scores(n, jblk + 1)

        issue_next(0)
        for n in range(nch):
            issue_next(n + 1)
            if masks[n] is False:
                continue
            st = st_ref[cur + n]
            if masks[n] is not None:
                st = jnp.where(masks[n], st, NEG_BIG)
            m_new, alpha, p = softmax_step(st, states[n][0])
            out[n] = (m_new, values(n, jblk, alpha, p, states[n][1]))
        return tuple(out)

    row = lax.broadcasted_iota(jnp.int32, (tk, tqs), 0)
    col = lax.broadcasted_iota(jnp.int32, (tk, tqs), 1)
    diag_masks = []
    for kb in range(nkb):
        masks = []
        for _, qs in chains:
            k_lo, q_lo = kb * tk, qs * tqs
            if k_lo > q_lo + tqs - 1:
                masks.append(False)
            else:
                masks.append(None if k_lo + tk - 1 <= q_lo else (row + k_lo <= col + q_lo))
        diag_masks.append(masks)
    no_mask = [None] * len(chains)

    states = tuple((jnp.full((1, tqs), NEG_BIG, F32), jnp.zeros((dh + FOX_DEN_ROWS, tqs), F32))
                   for _ in chains)
    @pl.when(i == 0)
    def _():
        for n in range(nch):
            st_ref[n] = scores(n, 0)

    def one_trip(jj, c):
        for u in range(FOX_BLOCKS_PER_TRIP):
            c = block(FOX_BLOCKS_PER_TRIP * jj + u, u % 2, c, no_mask, no_mask)
        return c

    states = lax.fori_loop(0, i * (nkb // FOX_BLOCKS_PER_TRIP), one_trip, states)
    for kb in range(nkb):
        states = block(i * nkb + kb, kb % 2, states, diag_masks[kb], diag_masks[kb + 1] if kb + 1 < nkb else None)
    def normalized(n):
        acc = states[n][1]
        return acc[:dh] / acc[dh:dh + 1]

    out_t = jnp.concatenate(
        [jnp.concatenate([normalized(h * nsub + qs) for qs in range(nsub)], axis=1) for h in range(2)],
        axis=0)
    o_ref[...] = out_t.T.astype(BF16)


def _fox_attn(qt, ka, vt):
    B, H, _, S = qt.shape
    tq, tqs, tk = min(FOX_TQ, S), min(FOX_TQS, S), min(FOX_TK, S)
    kern = functools.partial(_fox_attn_kernel, tq=tq, tqs=tqs, tk=tk)
    assert (tq // tk) % FOX_BLOCKS_PER_TRIP == 0 or S == tq
    last = S // tq - 1
    return pl.pallas_call(
        kern,
        out_shape=jax.ShapeDtypeStruct((B, S, D_MODEL), BF16),
        grid=(B, H // 2, S // tq),
        in_specs=[
            pl.BlockSpec((None, 2, LANES, tq), lambda b, p, i: (b, p, 0, i)),
            pl.BlockSpec((None, 2, LANES, tq), lambda b, p, i: (b, p, 0, jnp.minimum(i + 1, last))),
            pl.BlockSpec((None, 2, S, LANES), lambda b, p, i: (b, p, 0, 0)),
            pl.BlockSpec((None, None, S // tk, LANES, tk), lambda b, p, i: (b, p, 0, 0, 0)),
        ],
        out_specs=pl.BlockSpec((None, tq, LANES), lambda b, p, i: (b, i, p)),
        scratch_shapes=[pltpu.VMEM((2 * 2 * (tq // tqs), tk, tqs), F32)],
        compiler_params=_params("parallel", "parallel", "arbitrary"),
        name="fox_attn",
    )(qt, qt, ka, vt)


GLA_SCALE = GLA_DK_HEAD ** -0.5


def _gla_kernel(x_ref, g_ref, win_ref, wg1_ref, wg2_ref, bg_ref, og_ref, y_ref, st_ref, *, ts):
    @pl.when(pl.program_id(1) == 0)
    def _():
        st_ref[...] = jnp.zeros_like(st_ref)

    C = GLA_CHUNK
    hn = _rms(x_ref[...], g_ref[...]).astype(BF16)
    tri = (lax.broadcasted_iota(jnp.int32, (C, C), 0)
           >= lax.broadcasted_iota(jnp.int32, (C, C), 1))
    tri3 = jnp.concatenate([tri.astype(BF16)] * N_SPLIT, axis=1)
    og = og_ref[...]
    heads = range(GLA_HEADS)
    dks = [slice(h * GLA_DK_HEAD, (h + 1) * GLA_DK_HEAD) for h in heads]
    dvs = [slice(h * GLA_DV_HEAD, (h + 1) * GLA_DV_HEAD) for h in heads]
    proj = _dot(hn, win_ref[...])
    gate = _dot(_dot(hn, wg1_ref[...]).astype(BF16), wg2_ref[...]) + bg_ref[...]
    log_a = _log_sigmoid(gate) / GLA_TAU
    for c in range(ts // C):
        rows = slice(c * C, (c + 1) * C)
        b = _dot(tri3, jnp.concatenate(_split3(log_a[rows]), axis=0))
        b_last = b[C - 1:C, :]
        q_t = proj[rows, :GLA_DK] * GLA_SCALE * jnp.exp(b)
        kc = proj[rows, GLA_DK:2 * GLA_DK]
        k_t = kc * jnp.exp(-b)
        k_dec = kc * jnp.exp(b_last - b)
        decay = jnp.exp(b_last)
        qs = [q_t[:, dks[h]].astype(BF16) for h in heads]
        vs = [proj[rows, 2 * GLA_DK + h * GLA_DV_HEAD:2 * GLA_DK + (h + 1) * GLA_DV_HEAD].astype(BF16)
              for h in heads]
        atts = [lax.dot_general(qs[h], k_t[:, dks[h]].astype(BF16), _NT, preferred_element_type=F32)
                for h in heads]
        sts = [st_ref[h] for h in heads]
        inter = [lax.dot_general(qs[h], sts[h].astype(BF16), _NT, preferred_element_type=F32) for h in heads]
        for h in heads:
            st_ref[h] = sts[h] * decay[:, dks[h]] + lax.dot_general(vs[h], k_dec[:, dks[h]].astype(BF16), _TN,
                                                                    preferred_element_type=F32)
        for h in heads:
            o = _dot(jnp.where(tri, atts[h], 0.0).astype(BF16), vs[h]) + inter[h]
            on = o * lax.rsqrt(jnp.mean(o * o, axis=-1, keepdims=True) + RMS_EPS) * og
            r = proj[rows, 2 * GLA_DK + GLA_DV + h * GLA_DV_HEAD:2 * GLA_DK + GLA_DV + (h + 1) * GLA_DV_HEAD]
            y_ref[rows, dvs[h]] = (on * (r * _sigmoid(r))).astype(BF16)


def _gla(x, g, w_in, w_g1, w_g2, b_g, o_g):
    B, S, _ = x.shape
    ts = min(SEQ_TILE, S)
    kern = functools.partial(_gla_kernel, ts=ts)
    tile = pl.BlockSpec((None, ts, D_MODEL), lambda b, i: (b, i, 0))
    return pl.pallas_call(
        kern,
        out_shape=jax.ShapeDtypeStruct((B, S, D_MODEL), BF16),
        grid=(B, S // ts),
        in_specs=[tile, _const_spec((1, D_MODEL)), _const_spec((D_MODEL, 2 * GLA_DK + 2 * GLA_DV)),
                  _const_spec((D_MODEL, LANES)), _const_spec((LANES, GLA_DK)),
                  _const_spec((1, GLA_DK)), _const_spec((1, GLA_DV_HEAD))],
        out_specs=tile,
        scratch_shapes=[pltpu.VMEM((GLA_HEADS, GLA_DV_HEAD, GLA_DK_HEAD), F32)],
        compiler_params=_params("parallel", "arbitrary"),
        name="gla",
    )(x, g, w_in, w_g1, w_g2, b_g, o_g)


def _row(v):
    return v.reshape(1, -1).astype(F32)


def _pad_cols(w, n):
    return jnp.pad(w, ((0, 0), (0, n - w.shape[1])))


def kernel(x, mix_norm, mlp_norm, mlp_w1, mlp_w2,
           a_w_in, a_b_in, a_conv_w, a_conv_b, a_ln_g, a_ln_b, a_w_out, a_b_out,
           b_w_in, b_conv_w, b_w_out,
           c_w_qkv, c_q_norm, c_k_norm, c_w_f, c_b_f, c_w_out,
           d_w_in, d_w_g1, d_w_g2, d_b_g, d_o_norm, d_w_out):
    B, S, _ = x.shape
    depth = mix_norm.shape[0]
    zero_bias = jnp.zeros((1, D_MODEL), F32)
    w1_all, w2_all = mlp_w1.astype(BF16), mlp_w2.astype(BF16)
    for i in range(depth):
        m, j = i % 4, i // 4
        g = _row(mix_norm[i])
        if m == 0:
            y = _conformer(x, g, a_w_in[j].astype(BF16), _row(a_b_in[j]), a_conv_w[j], _row(a_conv_b[j]),
                           _row(a_ln_g[j]), _row(a_ln_b[j]))
            wo, bo = a_w_out[j], _row(a_b_out[j])
        elif m == 1:
            y = _short_conv(x, g, b_w_in[j].astype(BF16), b_conv_w[j])
            wo, bo = b_w_out[j], zero_bias
        elif m == 2:
            gq = _row(jnp.tile(c_q_norm[j], FOX_HEADS)) * (FOX_SCALE * LOG2E)
            gk = _row(jnp.tile(c_k_norm[j], FOX_HEADS))
            qt, ka, vt = _fox_proj(x, g, c_w_qkv[j].astype(BF16), _pad_cols(c_w_f[j], LANES).astype(BF16),
                                  _pad_cols(_row(c_b_f[j]), LANES), gq, gk)
            y = _fox_attn(qt, ka, vt)
            wo, bo = c_w_out[j], zero_bias
        else:
            w_g2 = jnp.pad(d_w_g2[j], ((0, LANES - GLA_GATE_RANK), (0, 0)))
            y = _gla(x, g, d_w_in[j].astype(BF16), _pad_cols(d_w_g1[j], LANES).astype(BF16),
                     w_g2.astype(BF16), _row(d_b_g[j]), _row(d_o_norm[j]))
            wo, bo = d_w_out[j], zero_bias
        x = _proj_mlp(x.reshape(B * S, D_MODEL), y.reshape(B * S, D_MODEL), wo.astype(BF16), bo,
                      _row(mlp_norm[i]), w1_all, w2_all, i).reshape(B, S, D_MODEL)
    return x
```

```python
import functools

import numpy as np
import jax
import jax.numpy as jnp
from jax import lax
from jax.experimental import pallas as pl
from jax.experimental.pallas import tpu as pltpu

F32 = jnp.float32
BF16 = jnp.bfloat16

D_MODEL = 1024
D_FF = 4 * D_MODEL
RMS_EPS = 1e-6
CONV_KERNEL = 31
SHORT_CONV = 3
FOX_HEADS = 16
FOX_HEAD_DIM = D_MODEL // FOX_HEADS
GLA_HEADS = 4
GLA_DK = D_MODEL // 2
GLA_DV = D_MODEL
GLA_DK_HEAD = GLA_DK // GLA_HEADS
GLA_DV_HEAD = GLA_DV // GLA_HEADS
GLA_GATE_RANK = 16
GLA_TAU = 16.0
GLA_CHUNK = 64

LANES = 128
SUBLANES = 8
CONV_ROWS = 64
CONV_HALO = 32
SHORT_HALO = 8
NEG_BIG = -1e30
VMEM_LIMIT = 56 * 1024 * 1024

_NT = (((1,), (1,)), ((), ()))
_TN = (((0,), (0,)), ((), ()))


def _dot(a, b):
    return jnp.dot(a, b, preferred_element_type=F32)


def _rms(x, g):
    return x * lax.rsqrt(jnp.mean(x * x, axis=-1, keepdims=True) + RMS_EPS) * g


def _sigmoid(x):
    return 0.5 * jnp.tanh(0.5 * x) + 0.5


def _log_sigmoid(x):
    return jnp.minimum(x, 0.0) - jnp.log1p(jnp.exp(-jnp.abs(x)))


def _split3(x):
    h1 = x.astype(BF16)
    r1 = x - h1.astype(F32)
    h2 = r1.astype(BF16)
    h3 = (r1 - h2.astype(F32)).astype(BF16)
    return h1, h2, h3


def _const_spec(shape):
    nd = len(shape)
    return pl.BlockSpec(shape, lambda *_: (0,) * nd, pipeline_mode=pl.Buffered(1))


def _params(*sem):
    return pltpu.CompilerParams(dimension_semantics=sem, vmem_limit_bytes=VMEM_LIMIT)


MLP_TM = 512
MLP_TF = 1024


def _mlp_kernel(x_ref, y_ref, wo_ref, bo_ref, g_ref, w1_ref, w2_ref, o_ref):
    x1 = x_ref[...] + _dot(y_ref[...], wo_ref[...]) + bo_ref[...]
    hn = _rms(x1, g_ref[...]).astype(BF16)
    acc = x1
    for c in range(D_FF // MLP_TF):
        cols = slice(c * MLP_TF, (c + 1) * MLP_TF)
        h = jnp.maximum(_dot(hn, w1_ref[:, cols]), 0.0)
        acc = acc + _dot((h * h).astype(BF16), w2_ref[cols, :])
    o_ref[...] = acc


def _layer_spec(shape, layer):
    nd = len(shape)
    return pl.BlockSpec((None,) + tuple(shape), lambda *_: (layer,) + (0,) * nd, pipeline_mode=pl.Buffered(1))


def _proj_mlp(x2d, y2d, wo, bo, g, w1_all, w2_all, layer):
    T = x2d.shape[0]
    tm = min(MLP_TM, T)
    row = lambda i: (i, 0)
    return pl.pallas_call(
        _mlp_kernel,
        out_shape=jax.ShapeDtypeStruct((T, D_MODEL), F32),
        grid=(T // tm,),
        in_specs=[
            pl.BlockSpec((tm, D_MODEL), row),
            pl.BlockSpec((tm, D_MODEL), row),
            _const_spec((D_MODEL, D_MODEL)),
            _const_spec((1, D_MODEL)),
            _const_spec((1, D_MODEL)),
            _layer_spec((D_MODEL, D_FF), layer),
            _layer_spec((D_FF, D_MODEL), layer),
        ],
        out_specs=pl.BlockSpec((tm, D_MODEL), row),
        compiler_params=_params("parallel"),
        name="proj_mlp",
    )(x2d, y2d, wo, bo, g, w1_all, w2_all)


SEQ_TILE = 512


def _conformer_kernel(xh_ref, x_ref, g_ref, win_ref, bin_ref, cw_ref, cb_ref, lg_ref, lb_ref,
                      y_ref, glu_ref, out_ref, *, ts):
    i = pl.program_id(1)
    xa = jnp.concatenate([xh_ref[...], x_ref[...]], axis=0)
    hn = _rms(xa, g_ref[...]).astype(BF16)
    p = _dot(hn, win_ref[...]) + bin_ref[...]
    glu = p[:, :D_MODEL] * _sigmoid(p[:, D_MODEL:])
    rows = lax.broadcasted_iota(jnp.int32, (CONV_HALO + ts, 1), 0)
    glu = jnp.where((rows >= CONV_HALO) | (i > 0), glu, 0.0)
    nlb = D_MODEL // LANES
    for lb in range(nlb):
        glu_ref[lb] = glu[:, lb * LANES:(lb + 1) * LANES]
    first = CONV_HALO - (CONV_KERNEL - 1)

    def lane_block(lb, carry):
        bias = cb_ref[lb]
        w = [cw_ref[lb, k] for k in range(CONV_KERNEL)]

        def row_chunk(rc, carry):
            r0 = pl.multiple_of(rc * CONV_ROWS, CONV_ROWS)
            accs = [bias] * SUBLANES
            for s in range(CONV_KERNEL + SUBLANES - 1):
                rows_s = glu_ref[lb, pl.ds(r0 + first + s, SUBLANES, stride=SUBLANES), :]
                for m in range(SUBLANES):
                    if 0 <= s - m < CONV_KERNEL:
                        accs[m] = accs[m] + w[s - m] * rows_s
            for m in range(SUBLANES):
                out_ref[lb, pl.ds(r0 + m, SUBLANES, stride=SUBLANES), :] = accs[m]
            return carry

        return lax.fori_loop(0, ts // CONV_ROWS, row_chunk, carry, unroll=2)

    lax.fori_loop(0, nlb, lane_block, 0)
    acc = jnp.concatenate([out_ref[lb] for lb in range(nlb)], axis=1)
    xc = acc - jnp.mean(acc, axis=-1, keepdims=True)
    y = xc * lax.rsqrt(jnp.mean(xc * xc, axis=-1, keepdims=True) + RMS_EPS)
    y = y * lg_ref[...] + lb_ref[...]
    y_ref[...] = (y * _sigmoid(y)).astype(BF16)


def _conformer(x, g, w_in, b_in, conv_w, conv_b, ln_g, ln_b):
    B, S, _ = x.shape
    ts = min(SEQ_TILE, S)
    per = ts // CONV_HALO
    kern = functools.partial(_conformer_kernel, ts=ts)
    nlb = D_MODEL // LANES
    conv_w = jnp.broadcast_to(conv_w.reshape(CONV_KERNEL, nlb, 1, LANES).transpose(1, 0, 2, 3),
                              (nlb, CONV_KERNEL, SUBLANES, LANES))
    conv_b = jnp.broadcast_to(conv_b.reshape(nlb, 1, LANES), (nlb, SUBLANES, LANES))
    return pl.pallas_call(
        kern,
        out_shape=jax.ShapeDtypeStruct((B, S, D_MODEL), BF16),
        grid=(B, S // ts),
        in_specs=[
            pl.BlockSpec((None, CONV_HALO, D_MODEL), lambda b, i: (b, jnp.maximum(i * per - 1, 0), 0)),
            pl.BlockSpec((None, ts, D_MODEL), lambda b, i: (b, i, 0)),
            _const_spec((1, D_MODEL)),
            _const_spec((D_MODEL, 2 * D_MODEL)),
            _const_spec((1, 2 * D_MODEL)),
            _const_spec((nlb, CONV_KERNEL, SUBLANES, LANES)),
            _const_spec((nlb, SUBLANES, LANES)),
            _const_spec((1, D_MODEL)),
            _const_spec((1, D_MODEL)),
        ],
        out_specs=pl.BlockSpec((None, ts, D_MODEL), lambda b, i: (b, i, 0)),
        scratch_shapes=[pltpu.VMEM((D_MODEL // LANES, CONV_HALO + ts, LANES), F32),
                        pltpu.VMEM((D_MODEL // LANES, ts, LANES), F32)],
        compiler_params=_params("parallel", "parallel"),
        name="conformer",
    )(x, x, g, w_in, b_in, conv_w, conv_b, ln_g, ln_b)


def _short_conv_kernel(xh_ref, x_ref, g_ref, win_ref, cw_ref, y_ref, u_ref, *, ts):
    i = pl.program_id(1)
    xa = jnp.concatenate([xh_ref[...], x_ref[...]], axis=0)
    hn = _rms(xa, g_ref[...]).astype(BF16)
    p = _dot(hn, win_ref[...])
    u = p[:, D_MODEL:2 * D_MODEL] * p[:, 2 * D_MODEL:]
    rows = lax.broadcasted_iota(jnp.int32, (SHORT_HALO + ts, 1), 0)
    u_ref[...] = jnp.where((rows >= SHORT_HALO) | (i > 0), u, 0.0)
    first = SHORT_HALO - (SHORT_CONV - 1)
    conv = cw_ref[0:1, :] * u_ref[pl.ds(first, ts), :]
    for k in range(1, SHORT_CONV):
        conv = conv + cw_ref[k:k + 1, :] * u_ref[pl.ds(first + k, ts), :]
    y_ref[...] = (p[SHORT_HALO:, :D_MODEL] * conv).astype(BF16)


def _short_conv(x, g, w_in, conv_w):
    B, S, _ = x.shape
    ts = min(SEQ_TILE, S)
    per = ts // SHORT_HALO
    kern = functools.partial(_short_conv_kernel, ts=ts)
    return pl.pallas_call(
        kern,
        out_shape=jax.ShapeDtypeStruct((B, S, D_MODEL), BF16),
        grid=(B, S // ts),
        in_specs=[
            pl.BlockSpec((None, SHORT_HALO, D_MODEL), lambda b, i: (b, jnp.maximum(i * per - 1, 0), 0)),
            pl.BlockSpec((None, ts, D_MODEL), lambda b, i: (b, i, 0)),
            _const_spec((1, D_MODEL)),
            _const_spec((D_MODEL, 3 * D_MODEL)),
            _const_spec((SHORT_CONV, D_MODEL)),
        ],
        out_specs=pl.BlockSpec((None, ts, D_MODEL), lambda b, i: (b, i, 0)),
        scratch_shapes=[pltpu.VMEM((SHORT_HALO + ts, D_MODEL), F32)],
        compiler_params=_params("parallel", "parallel"),
        name="short_conv",
    )(x, x, g, w_in, conv_w)


FOX_SCALE = FOX_HEAD_DIM ** -0.5
LOG2E = 1.4426950408889634
N_SPLIT = 3
FOX_TQ = 2048
FOX_BLOCKS_PER_TRIP = 4
FOX_DEN_ROWS = 16
FOX_TQS = 256
FOX_TK = 256


def _fox_constants():
    H, dh = FOX_HEADS, FOX_HEAD_DIM
    p_q = np.zeros((LANES, D_MODEL), np.float32)
    p_k = np.zeros((LANES, D_MODEL), np.float32)
    one_q = np.zeros((1, D_MODEL), np.float32)
    one_k = np.zeros((1, D_MODEL), np.float32)
    for h in range(H):
        base = (h // 2) * LANES + (dh if h % 2 == 0 else 0)
        for s in range(N_SPLIT):
            p_q[s * H + h, base + s] = 1.0
            one_q[0, base + N_SPLIT + s] = 1.0
            one_k[0, base + s] = 1.0
            p_k[s * H + h, base + N_SPLIT + s] = -1.0
    return jnp.asarray(p_q, BF16), jnp.asarray(p_k, BF16), jnp.asarray(one_q), jnp.asarray(one_k)


def _pack_pieces(pieces):
    out = pieces[0].astype(F32)
    for s in range(1, N_SPLIT):
        out = out + pltpu.roll(pieces[s].astype(F32), s * FOX_HEADS, 1)
    return out.astype(BF16)


def _pair_rms(blk, gain, low):
    sq = blk * blk
    s_low = jnp.sum(jnp.where(low, sq, 0.0), axis=-1, keepdims=True)
    s_high = jnp.sum(jnp.where(low, 0.0, sq), axis=-1, keepdims=True)
    inv = jnp.where(low, lax.rsqrt(s_low * (1.0 / FOX_HEAD_DIM) + RMS_EPS),
                    lax.rsqrt(s_high * (1.0 / FOX_HEAD_DIM) + RMS_EPS))
    return blk * inv * gain


def _fox_proj_kernel(x_ref, g_ref, wqkv_ref, wf_ref, bf_ref, gq_ref, gk_ref,
                     pq_ref, pk_ref, oneq_ref, onek_ref, qt_ref, ka_ref, vt_ref, carry_ref, *, ts, tk):
    @pl.when(pl.program_id(1) == 0)
    def _():
        carry_ref[...] = jnp.zeros_like(carry_ref)

    hn = _rms(x_ref[...], g_ref[...]).astype(BF16)
    qkv = _dot(hn, wqkv_ref[...])

    lane = lax.broadcasted_iota(jnp.int32, (1, LANES), 1)
    logf = _log_sigmoid(_dot(hn, wf_ref[...]) + bf_ref[...])
    logf = jnp.where(lane < FOX_HEADS, logf, 0.0)
    tri = (lax.broadcasted_iota(jnp.int32, (ts, ts), 0)
           >= lax.broadcasted_iota(jnp.int32, (ts, ts), 1)).astype(BF16)
    cs = _dot(tri, _pack_pieces(_split3(logf)))
    c = cs + pltpu.roll(cs, LANES - FOX_HEADS, 1) + pltpu.roll(cs, LANES - 2 * FOX_HEADS, 1)
    c = jnp.where(lane < FOX_HEADS, c, 0.0) + carry_ref[...]
    carry_ref[...] = c[ts - 1:ts, :]
    packed = _pack_pieces(_split3(c * LOG2E))
    aug_q = _dot(packed, pq_ref[...]) + oneq_ref[...]
    aug_k = _dot(packed, pk_ref[...]) + onek_ref[...]

    low = lane < FOX_HEAD_DIM
    for p in range(FOX_HEADS // 2):
        blk = slice(p * LANES, (p + 1) * LANES)
        qn = _pair_rms(qkv[:, blk], gq_ref[:, blk], low)
        kn = _pair_rms(qkv[:, D_MODEL + p * LANES:D_MODEL + (p + 1) * LANES], gk_ref[:, blk], low)
        qt_ref[2 * p] = jnp.where(low, qn, aug_q[:, blk]).T.astype(BF16)
        qt_ref[2 * p + 1] = jnp.where(low, aug_q[:, blk], qn).T.astype(BF16)
        ka_ref[2 * p] = jnp.where(low, kn, aug_k[:, blk]).astype(BF16)
        ka_ref[2 * p + 1] = jnp.where(low, aug_k[:, blk], kn).astype(BF16)
        vt = qkv[:, 2 * D_MODEL + p * LANES:2 * D_MODEL + (p + 1) * LANES].T
        for jj in range(ts // tk):
            vt_ref[p, jj] = vt[:, jj * tk:(jj + 1) * tk].astype(BF16)


def _fox_proj(x, g, w_qkv, w_f, b_f, gq, gk):
    B, S, _ = x.shape
    ts = min(SEQ_TILE, S)
    tk = min(FOX_TK, S)
    consts = _fox_constants()
    kern = functools.partial(_fox_proj_kernel, ts=ts, tk=tk)
    tile = pl.BlockSpec((None, ts, D_MODEL), lambda b, i: (b, i, 0))
    return pl.pallas_call(
        kern,
        out_shape=(jax.ShapeDtypeStruct((B, FOX_HEADS, LANES, S), BF16),
                   jax.ShapeDtypeStruct((B, FOX_HEADS, S, LANES), BF16),
                   jax.ShapeDtypeStruct((B, FOX_HEADS // 2, S // tk, LANES, tk), BF16)),
        grid=(B, S // ts),
        in_specs=[tile, _const_spec((1, D_MODEL)), _const_spec((D_MODEL, 3 * D_MODEL)),
                  _const_spec((D_MODEL, LANES)), _const_spec((1, LANES)),
                  _const_spec((1, D_MODEL)), _const_spec((1, D_MODEL))]
                 + [_const_spec(c.shape) for c in consts],
        out_specs=(pl.BlockSpec((None, FOX_HEADS, LANES, ts), lambda b, i: (b, 0, 0, i)),
                   pl.BlockSpec((None, FOX_HEADS, ts, LANES), lambda b, i: (b, 0, i, 0)),
                   pl.BlockSpec((None, FOX_HEADS // 2, ts // tk, LANES, tk), lambda b, i: (b, 0, i, 0, 0))),
        scratch_shapes=[pltpu.VMEM((1, LANES), F32)],
        compiler_params=_params("parallel", "arbitrary"),
        name="fox_proj",
    )(x, g, w_qkv, w_f, b_f, gq, gk, *consts)


def _fox_attn_kernel(qt_ref, qt_next_ref, k_ref, vt_ref, o_ref, st_ref, *, tq, tqs, tk):
    i = pl.program_id(2)
    nsub = tq // tqs
    nkb = tq // tk
    dh = FOX_HEAD_DIM
    chains = [(h, qs) for h in range(2) for qs in range(nsub)]

    def scores(n, jblk, q_ref=qt_ref):
        h, qs = chains[n]
        koff = pl.multiple_of(jblk * tk, tk)
        return _dot(k_ref[h, pl.ds(koff, tk), :], q_ref[h, :, qs * tqs:(qs + 1) * tqs])

    def softmax_step(st, m):
        m_new = jnp.maximum(m, jnp.max(st, axis=0, keepdims=True))
        return m_new, jnp.exp2(m - m_new), jnp.exp2(st - m_new).astype(BF16)

    ones_rows = jnp.ones((FOX_DEN_ROWS, tk), BF16)

    def values(n, jblk, alpha, p, a):
        h, _ = chains[n]
        vt = jnp.concatenate([vt_ref[jblk, h * dh:(h + 1) * dh, :], ones_rows], axis=0)
        return alpha * a + _dot(vt, p)

    nch = len(chains)

    def block(jblk, slot, states, masks, next_masks):
        cur = slot * nch
        nxt = nch - cur
        out = list(states)

        def issue_next(n):
            if n >= nch:
                return
            if next_masks is None:
                st_ref[nxt + n] = scores(n, 0, qt_next_ref)
            elif next_masks[n] is not False:
                st_ref[nxt + n] = scores(n, jblk + 1)

        issue_next(0)
        for n in range(nch):
            issue_next(n + 1)
            if masks[n] is False:
                continue
            st = st_ref[cur + n]
            if masks[n] is not None:
                st = jnp.where(masks[n], st, NEG_BIG)
            m_new, alpha, p = softmax_step(st, states[n][0])
            out[n] = (m_new, values(n, jblk, alpha, p, states[n][1]))
        return tuple(out)

    row = lax.broadcasted_iota(jnp.int32, (tk, tqs), 0)
    col = lax.broadcasted_iota(jnp.int32, (tk, tqs), 1)
    diag_masks = []
    for kb in range(nkb):
        masks = []
        for _, qs in chains:
            k_lo, q_lo = kb * tk, qs * tqs
            if k_lo > q_lo + tqs - 1:
                masks.append(False)
            else:
                masks.append(None if k_lo + tk - 1 <= q_lo else (row + k_lo <= col + q_lo))
        diag_masks.append(masks)
    no_mask = [None] * len(chains)

    states = tuple((jnp.full((1, tqs), NEG_BIG, F32), jnp.zeros((dh + FOX_DEN_ROWS, tqs), F32))
                   for _ in chains)

    @pl.when(i == 0)
    def _():
        for n in range(nch):
            st_ref[n] = scores(n, 0)

    def one_trip(jj, c):
        for u in range(FOX_BLOCKS_PER_TRIP):
            c = block(FOX_BLOCKS_PER_TRIP * jj + u, u % 2, c, no_mask, no_mask)
        return c

    states = lax.fori_loop(0, i * (nkb // FOX_BLOCKS_PER_TRIP), one_trip, states)
    for kb in range(nkb):
        states = block(i * nkb + kb, kb % 2, states, diag_masks[kb], diag_masks[kb + 1] if kb + 1 < nkb else None)

    def normalized(n):
        acc = states[n][1]
        return acc[:dh] / acc[dh:dh + 1]

    out_t = jnp.concatenate(
        [jnp.concatenate([normalized(h * nsub + qs) for qs in range(nsub)], axis=1) for h in range(2)],
        axis=0)
    o_ref[...] = out_t.T.astype(BF16)


def _fox_attn(qt, ka, vt):
    B, H, _, S = qt.shape
    tq, tqs, tk = min(FOX_TQ, S), min(FOX_TQS, S), min(FOX_TK, S)
    kern = functools.partial(_fox_attn_kernel, tq=tq, tqs=tqs, tk=tk)
    assert S % tq == 0 and tq % tqs == 0 and (tq // tk) % FOX_BLOCKS_PER_TRIP == 0
    last = S // tq - 1
    return pl.pallas_call(
        kern,
        out_shape=jax.ShapeDtypeStruct((B, S, D_MODEL), BF16),
        grid=(B, H // 2, S // tq),
        in_specs=[
            pl.BlockSpec((None, 2, LANES, tq), lambda b, p, i: (b, p, 0, i)),
            pl.BlockSpec((None, 2, LANES, tq), lambda b, p, i: (b, p, 0, jnp.minimum(i + 1, last))),
            pl.BlockSpec((None, 2, S, LANES), lambda b, p, i: (b, p, 0, 0)),
            pl.BlockSpec((None, None, S // tk, LANES, tk), lambda b, p, i: (b, p, 0, 0, 0)),
        ],
        out_specs=pl.BlockSpec((None, tq, LANES), lambda b, p, i: (b, i, p)),
        scratch_shapes=[pltpu.VMEM((2 * 2 * (tq // tqs), tk, tqs), F32)],
        compiler_params=_params("parallel", "parallel", "arbitrary"),
        name="fox_attn",
    )(qt, qt, ka, vt)


GLA_SCALE = GLA_DK_HEAD ** -0.5


def _gla_kernel(x_ref, g_ref, win_ref, wg1_ref, wg2_ref, bg_ref, og_ref, y_ref, st_ref, *, ts):
    @pl.when(pl.program_id(1) == 0)
    def _():
        st_ref[...] = jnp.zeros_like(st_ref)

    C = GLA_CHUNK
    hn = _rms(x_ref[...], g_ref[...]).astype(BF16)
    tri = (lax.broadcasted_iota(jnp.int32, (C, C), 0)
           >= lax.broadcasted_iota(jnp.int32, (C, C), 1))
    tri3 = jnp.concatenate([tri.astype(BF16)] * N_SPLIT, axis=1)
    og = og_ref[...]
    heads = range(GLA_HEADS)
    dks = [slice(h * GLA_DK_HEAD, (h + 1) * GLA_DK_HEAD) for h in heads]
    dvs = [slice(h * GLA_DV_HEAD, (h + 1) * GLA_DV_HEAD) for h in heads]
    proj = _dot(hn, win_ref[...])
    gate = _dot(_dot(hn, wg1_ref[...]).astype(BF16), wg2_ref[...]) + bg_ref[...]
    log_a = _log_sigmoid(gate) / GLA_TAU
    for c in range(ts // C):
        rows = slice(c * C, (c + 1) * C)
        b = _dot(tri3, jnp.concatenate(_split3(log_a[rows]), axis=0))
        b_last = b[C - 1:C, :]
        q_t = proj[rows, :GLA_DK] * GLA_SCALE * jnp.exp(b)
        kc = proj[rows, GLA_DK:2 * GLA_DK]
        k_t = kc * jnp.exp(-b)
        k_dec = kc * jnp.exp(b_last - b)
        decay = jnp.exp(b_last)
        qs = [q_t[:, dks[h]].astype(BF16) for h in heads]
        vs = [proj[rows, 2 * GLA_DK + h * GLA_DV_HEAD:2 * GLA_DK + (h + 1) * GLA_DV_HEAD].astype(BF16)
              for h in heads]
        atts = [lax.dot_general(qs[h], k_t[:, dks[h]].astype(BF16), _NT, preferred_element_type=F32)
                for h in heads]
        sts = [st_ref[h] for h in heads]
        inter = [lax.dot_general(qs[h], sts[h].astype(BF16), _NT, preferred_element_type=F32) for h in heads]
        for h in heads:
            st_ref[h] = sts[h] * decay[:, dks[h]] + lax.dot_general(vs[h], k_dec[:, dks[h]].astype(BF16), _TN,
                                                                    preferred_element_type=F32)
        for h in heads:
            o = _dot(jnp.where(tri, atts[h], 0.0).astype(BF16), vs[h]) + inter[h]
            on = o * lax.rsqrt(jnp.mean(o * o, axis=-1, keepdims=True) + RMS_EPS) * og
            r = proj[rows, 2 * GLA_DK + GLA_DV + h * GLA_DV_HEAD:2 * GLA_DK + GLA_DV + (h + 1) * GLA_DV_HEAD]
            y_ref[rows, dvs[h]] = (on * (r * _sigmoid(r))).astype(BF16)


def _gla(x, g, w_in, w_g1, w_g2, b_g, o_g):
    B, S, _ = x.shape
    ts = min(SEQ_TILE, S)
    kern = functools.partial(_gla_kernel, ts=ts)
    tile = pl.BlockSpec((None, ts, D_MODEL), lambda b, i: (b, i, 0))
    return pl.pallas_call(
        kern,
        out_shape=jax.ShapeDtypeStruct((B, S, D_MODEL), BF16),
        grid=(B, S // ts),
        in_specs=[tile, _const_spec((1, D_MODEL)), _const_spec((D_MODEL, 2 * GLA_DK + 2 * GLA_DV)),
                  _const_spec((D_MODEL, LANES)), _const_spec((LANES, GLA_DK)),
                  _const_spec((1, GLA_DK)), _const_spec((1, GLA_DV_HEAD))],
        out_specs=tile,
        scratch_shapes=[pltpu.VMEM((GLA_HEADS, GLA_DV_HEAD, GLA_DK_HEAD), F32)],
        compiler_params=_params("parallel", "arbitrary"),
        name="gla",
    )(x, g, w_in, w_g1, w_g2, b_g, o_g)


def _row(v):
    return v.reshape(1, -1).astype(F32)


def _pad_cols(w, n):
    return jnp.pad(w, ((0, 0), (0, n - w.shape[1])))


def kernel(x, mix_norm, mlp_norm, mlp_w1, mlp_w2,
           a_w_in, a_b_in, a_conv_w, a_conv_b, a_ln_g, a_ln_b, a_w_out, a_b_out,
           b_w_in, b_conv_w, b_w_out,
           c_w_qkv, c_q_norm, c_k_norm, c_w_f, c_b_f, c_w_out,
           d_w_in, d_w_g1, d_w_g2, d_b_g, d_o_norm, d_w_out):
    B, S, _ = x.shape
    depth = mix_norm.shape[0]
    zero_bias = jnp.zeros((1, D_MODEL), F32)
    w1_all, w2_all = mlp_w1.astype(BF16), mlp_w2.astype(BF16)
    for i in range(depth):
        m, j = i % 4, i // 4
        g = _row(mix_norm[i])
        if m == 0:
            y = _conformer(x, g, a_w_in[j].astype(BF16), _row(a_b_in[j]), a_conv_w[j], _row(a_conv_b[j]),
                           _row(a_ln_g[j]), _row(a_ln_b[j]))
            wo, bo = a_w_out[j], _row(a_b_out[j])
        elif m == 1:
            y = _short_conv(x, g, b_w_in[j].astype(BF16), b_conv_w[j])
            wo, bo = b_w_out[j], zero_bias
        elif m == 2:
            gq = _row(jnp.tile(c_q_norm[j], FOX_HEADS)) * (FOX_SCALE * LOG2E)
            gk = _row(jnp.tile(c_k_norm[j], FOX_HEADS))
            qt, ka, vt = _fox_proj(x, g, c_w_qkv[j].astype(BF16), _pad_cols(c_w_f[j], LANES).astype(BF16),
                                  _pad_cols(_row(c_b_f[j]), LANES), gq, gk)
            y = _fox_attn(qt, ka, vt)
            wo, bo = c_w_out[j], zero_bias
        else:
            w_g2 = jnp.pad(d_w_g2[j], ((0, LANES - GLA_GATE_RANK), (0, 0)))
            y = _gla(x, g, d_w_in[j].astype(BF16), _pad_cols(d_w_g1[j], LANES).astype(BF16),
                     w_g2.astype(BF16), _row(d_b_g[j]), _row(d_o_norm[j]))
            wo, bo = d_w_out[j], zero_bias
        x = _proj_mlp(x.reshape(B * S, D_MODEL), y.reshape(B * S, D_MODEL), wo.astype(BF16), bo,
                      _row(mlp_norm[i]), w1_all, w2_all, i).reshape(B, S, D_MODEL)
    return x
```

```python
import functools

import numpy as np
import jax
import jax.numpy as jnp
from jax import lax
from jax.experimental import pallas as pl
from jax.experimental.pallas import tpu as pltpu

F32 = jnp.float32
BF16 = jnp.bfloat16

D_MODEL = 1024
D_FF = 4 * D_MODEL
RMS_EPS = 1e-6
CONV_KERNEL = 31
SHORT_CONV = 3
FOX_HEADS = 16
FOX_HEAD_DIM = D_MODEL // FOX_HEADS
GLA_HEADS = 4
GLA_DK = D_MODEL // 2
GLA_DV = D_MODEL
GLA_DK_HEAD = GLA_DK // GLA_HEADS
GLA_DV_HEAD = GLA_DV // GLA_HEADS
GLA_GATE_RANK = 16
GLA_TAU = 16.0
GLA_CHUNK = 64

LANES = 128
SUBLANES = 8
CONV_ROWS = 64
CONV_HALO = 32
SHORT_HALO = 8
NEG_BIG = -1e30
VMEM_LIMIT = 56 * 1024 * 1024

_NT = (((1,), (1,)), ((), ()))
_TN = (((0,), (0,)), ((), ()))


def _dot(a, b):
    return jnp.dot(a, b, preferred_element_type=F32)


def _rms(x, g):
    return x * lax.rsqrt(jnp.mean(x * x, axis=-1, keepdims=True) + RMS_EPS) * g


def _sigmoid(x):
    return 0.5 * jnp.tanh(0.5 * x) + 0.5


def _log_sigmoid(x):
    return jnp.minimum(x, 0.0) - jnp.log1p(jnp.exp(-jnp.abs(x)))


def _split3(x):
    h1 = x.astype(BF16)
    r1 = x - h1.astype(F32)
    h2 = r1.astype(BF16)
    h3 = (r1 - h2.astype(F32)).astype(BF16)
    return h1, h2, h3


def _const_spec(shape):
    nd = len(shape)
    return pl.BlockSpec(shape, lambda *_: (0,) * nd, pipeline_mode=pl.Buffered(1))


def _params(*sem):
    return pltpu.CompilerParams(dimension_semantics=sem, vmem_limit_bytes=VMEM_LIMIT)


MLP_TM = 512
MLP_TF = 1024


def _mlp_kernel(x_ref, y_ref, wo_ref, bo_ref, g_ref, w1_ref, w2_ref, o_ref):
    x1 = x_ref[...] + _dot(y_ref[...], wo_ref[...]) + bo_ref[...]
    hn = _rms(x1, g_ref[...]).astype(BF16)
    acc = x1
    for c in range(D_FF // MLP_TF):
        cols = slice(c * MLP_TF, (c + 1) * MLP_TF)
        h = jnp.maximum(_dot(hn, w1_ref[:, cols]), 0.0)
        acc = acc + _dot((h * h).astype(BF16), w2_ref[cols, :])
    o_ref[...] = acc


def _layer_spec(shape, layer):
    nd = len(shape)
    return pl.BlockSpec((None,) + tuple(shape), lambda *_: (layer,) + (0,) * nd, pipeline_mode=pl.Buffered(1))


def _proj_mlp(x2d, y2d, wo, bo, g, w1_all, w2_all, layer):
    T = x2d.shape[0]
    tm = min(MLP_TM, T)
    row = lambda i: (i, 0)
    return pl.pallas_call(
        _mlp_kernel,
        out_shape=jax.ShapeDtypeStruct((T, D_MODEL), F32),
        grid=(T // tm,),
        in_specs=[
            pl.BlockSpec((tm, D_MODEL), row),
            pl.BlockSpec((tm, D_MODEL), row),
            _const_spec((D_MODEL, D_MODEL)),
            _const_spec((1, D_MODEL)),
            _const_spec((1, D_MODEL)),
            _layer_spec((D_MODEL, D_FF), layer),
            _layer_spec((D_FF, D_MODEL), layer),
        ],
        out_specs=pl.BlockSpec((tm, D_MODEL), row),
        compiler_params=_params("parallel"),
        name="proj_mlp",
    )(x2d, y2d, wo, bo, g, w1_all, w2_all)


SEQ_TILE = 512


def _conformer_kernel(xh_ref, x_ref, g_ref, win_ref, bin_ref, cw_ref, cb_ref, lg_ref, lb_ref,
                      y_ref, glu_ref, out_ref, *, ts):
    i = pl.program_id(1)
    xa = jnp.concatenate([xh_ref[...], x_ref[...]], axis=0)
    hn = _rms(xa, g_ref[...]).astype(BF16)
    p = _dot(hn, win_ref[...]) + bin_ref[...]
    glu = p[:, :D_MODEL] * _sigmoid(p[:, D_MODEL:])
    rows = lax.broadcasted_iota(jnp.int32, (CONV_HALO + ts, 1), 0)
    glu = jnp.where((rows >= CONV_HALO) | (i > 0), glu, 0.0)
    nlb = D_MODEL // LANES
    for lb in range(nlb):
        glu_ref[lb] = glu[:, lb * LANES:(lb + 1) * LANES]
    first = CONV_HALO - (CONV_KERNEL - 1)

    def lane_block(lb, carry):
        bias = cb_ref[lb]
        w = [cw_ref[lb, k] for k in range(CONV_KERNEL)]

        def row_chunk(rc, carry):
            r0 = pl.multiple_of(rc * CONV_ROWS, CONV_ROWS)
            accs = [bias] * SUBLANES
            for s in range(CONV_KERNEL + SUBLANES - 1):
                rows_s = glu_ref[lb, pl.ds(r0 + first + s, SUBLANES, stride=SUBLANES), :]
                for m in range(SUBLANES):
                    if 0 <= s - m < CONV_KERNEL:
                        accs[m] = accs[m] + w[s - m] * rows_s
            for m in range(SUBLANES):
                out_ref[lb, pl.ds(r0 + m, SUBLANES, stride=SUBLANES), :] = accs[m]
            return carry

        return lax.fori_loop(0, ts // CONV_ROWS, row_chunk, carry, unroll=2)

    lax.fori_loop(0, nlb, lane_block, 0)
    acc = jnp.concatenate([out_ref[lb] for lb in range(nlb)], axis=1)
    xc = acc - jnp.mean(acc, axis=-1, keepdims=True)
    y = xc * lax.rsqrt(jnp.mean(xc * xc, axis=-1, keepdims=True) + RMS_EPS)
    y = y * lg_ref[...] + lb_ref[...]
    y_ref[...] = (y * _sigmoid(y)).astype(BF16)


def _conformer(x, g, w_in, b_in, conv_w, conv_b, ln_g, ln_b):
    B, S, _ = x.shape
    ts = min(SEQ_TILE, S)
    per = ts // CONV_HALO
    kern = functools.partial(_conformer_kernel, ts=ts)
    nlb = D_MODEL // LANES
    conv_w = jnp.broadcast_to(conv_w.reshape(CONV_KERNEL, nlb, 1, LANES).transpose(1, 0, 2, 3),
                              (nlb, CONV_KERNEL, SUBLANES, LANES))
    conv_b = jnp.broadcast_to(conv_b.reshape(nlb, 1, LANES), (nlb, SUBLANES, LANES))
    return pl.pallas_call(
        kern,
        out_shape=jax.ShapeDtypeStruct((B, S, D_MODEL), BF16),
        grid=(B, S // ts),
        in_specs=[
            pl.BlockSpec((None, CONV_HALO, D_MODEL), lambda b, i: (b, jnp.maximum(i * per - 1, 0), 0)),
            pl.BlockSpec((None, ts, D_MODEL), lambda b, i: (b, i, 0)),
            _const_spec((1, D_MODEL)),
            _const_spec((D_MODEL, 2 * D_MODEL)),
            _const_spec((1, 2 * D_MODEL)),
            _const_spec((nlb, CONV_KERNEL, SUBLANES, LANES)),
            _const_spec((nlb, SUBLANES, LANES)),
            _const_spec((1, D_MODEL)),
            _const_spec((1, D_MODEL)),
        ],
        out_specs=pl.BlockSpec((None, ts, D_MODEL), lambda b, i: (b, i, 0)),
        scratch_shapes=[pltpu.VMEM((D_MODEL // LANES, CONV_HALO + ts, LANES), F32),
                        pltpu.VMEM((D_MODEL // LANES, ts, LANES), F32)],
        compiler_params=_params("parallel", "parallel"),
        name="conformer",
    )(x, x, g, w_in, b_in, conv_w, conv_b, ln_g, ln_b)


def _short_conv_kernel(xh_ref, x_ref, g_ref, win_ref, cw_ref, y_ref, u_ref, *, ts):
    i = pl.program_id(1)
    xa = jnp.concatenate([xh_ref[...], x_ref[...]], axis=0)
    hn = _rms(xa, g_ref[...]).astype(BF16)
    p = _dot(hn, win_ref[...])
    u = p[:, D_MODEL:2 * D_MODEL] * p[:, 2 * D_MODEL:]
    rows = lax.broadcasted_iota(jnp.int32, (SHORT_HALO + ts, 1), 0)
    u_ref[...] = jnp.where((rows >= SHORT_HALO) | (i > 0), u, 0.0)
    first = SHORT_HALO - (SHORT_CONV - 1)
    conv = cw_ref[0:1, :] * u_ref[pl.ds(first, ts), :]
    for k in range(1, SHORT_CONV):
        conv = conv + cw_ref[k:k + 1, :] * u_ref[pl.ds(first + k, ts), :]
    y_ref[...] = (p[SHORT_HALO:, :D_MODEL] * conv).astype(BF16)


def _short_conv(x, g, w_in, conv_w):
    B, S, _ = x.shape
    ts = min(SEQ_TILE, S)
    per = ts // SHORT_HALO
    kern = functools.partial(_short_conv_kernel, ts=ts)
    return pl.pallas_call(
        kern,
        out_shape=jax.ShapeDtypeStruct((B, S, D_MODEL), BF16),
        grid=(B, S // ts),
        in_specs=[
            pl.BlockSpec((None, SHORT_HALO, D_MODEL), lambda b, i: (b, jnp.maximum(i * per - 1, 0), 0)),
            pl.BlockSpec((None, ts, D_MODEL), lambda b, i: (b, i, 0)),
            _const_spec((1, D_MODEL)),
            _const_spec((D_MODEL, 3 * D_MODEL)),
            _const_spec((SHORT_CONV, D_MODEL)),
        ],
        out_specs=pl.BlockSpec((None, ts, D_MODEL), lambda b, i: (b, i, 0)),
        scratch_shapes=[pltpu.VMEM((SHORT_HALO + ts, D_MODEL), F32)],
        compiler_params=_params("parallel", "parallel"),
        name="short_conv",
    )(x, x, g, w_in, conv_w)


FOX_SCALE = FOX_HEAD_DIM ** -0.5
LOG2E = 1.4426950408889634
N_SPLIT = 3
FOX_TQ = 2048
FOX_BLOCKS_PER_TRIP = 8
FOX_DEN_ROWS = 16
FOX_TQS = 256
FOX_TK = 256


def _fox_constants():
    H, dh = FOX_HEADS, FOX_HEAD_DIM
    p_q = np.zeros((LANES, D_MODEL), np.float32)
    p_k = np.zeros((LANES, D_MODEL), np.float32)
    one_q = np.zeros((1, D_MODEL), np.float32)
    one_k = np.zeros((1, D_MODEL), np.float32)
    for h in range(H):
        base = (h // 2) * LANES + (dh if h % 2 == 0 else 0)
        for s in range(N_SPLIT):
            p_q[s * H + h, base + s] = 1.0
            one_q[0, base + N_SPLIT + s] = 1.0
            one_k[0, base + s] = 1.0
            p_k[s * H + h, base + N_SPLIT + s] = -1.0
    return jnp.asarray(p_q, BF16), jnp.asarray(p_k, BF16), jnp.asarray(one_q), jnp.asarray(one_k)


def _pack_pieces(pieces):
    out = pieces[0].astype(F32)
    for s in range(1, N_SPLIT):
        out = out + pltpu.roll(pieces[s].astype(F32), s * FOX_HEADS, 1)
    return out.astype(BF16)


def _pair_rms(blk, gain, low):
    sq = blk * blk
    s_low = jnp.sum(jnp.where(low, sq, 0.0), axis=-1, keepdims=True)
    s_high = jnp.sum(jnp.where(low, 0.0, sq), axis=-1, keepdims=True)
    inv = jnp.where(low, lax.rsqrt(s_low * (1.0 / FOX_HEAD_DIM) + RMS_EPS),
                    lax.rsqrt(s_high * (1.0 / FOX_HEAD_DIM) + RMS_EPS))
    return blk * inv * gain


def _fox_proj_kernel(x_ref, g_ref, wqkv_ref, wf_ref, bf_ref, gq_ref, gk_ref,
                     pq_ref, pk_ref, oneq_ref, onek_ref, qt_ref, ka_ref, vt_ref, carry_ref, *, ts, tk):
    @pl.when(pl.program_id(1) == 0)
    def _():
        carry_ref[...] = jnp.zeros_like(carry_ref)

    hn = _rms(x_ref[...], g_ref[...]).astype(BF16)
    qkv = _dot(hn, wqkv_ref[...])

    lane = lax.broadcasted_iota(jnp.int32, (1, LANES), 1)
    logf = _log_sigmoid(_dot(hn, wf_ref[...]) + bf_ref[...])
    logf = jnp.where(lane < FOX_HEADS, logf, 0.0)
    tri = (lax.broadcasted_iota(jnp.int32, (ts, ts), 0)
           >= lax.broadcasted_iota(jnp.int32, (ts, ts), 1)).astype(BF16)
    cs = _dot(tri, _pack_pieces(_split3(logf)))
    c = cs + pltpu.roll(cs, LANES - FOX_HEADS, 1) + pltpu.roll(cs, LANES - 2 * FOX_HEADS, 1)
    c = jnp.where(lane < FOX_HEADS, c, 0.0) + carry_ref[...]
    carry_ref[...] = c[ts - 1:ts, :]
    packed = _pack_pieces(_split3(c * LOG2E))
    aug_q = _dot(packed, pq_ref[...]) + oneq_ref[...]
    aug_k = _dot(packed, pk_ref[...]) + onek_ref[...]

    low = lane < FOX_HEAD_DIM
    for p in range(FOX_HEADS // 2):
        blk = slice(p * LANES, (p + 1) * LANES)
        qn = _pair_rms(qkv[:, blk], gq_ref[:, blk], low)
        kn = _pair_rms(qkv[:, D_MODEL + p * LANES:D_MODEL + (p + 1) * LANES], gk_ref[:, blk], low)
        qt_ref[2 * p] = jnp.where(low, qn, aug_q[:, blk]).T.astype(BF16)
        qt_ref[2 * p + 1] = jnp.where(low, aug_q[:, blk], qn).T.astype(BF16)
        ka_ref[2 * p] = jnp.where(low, kn, aug_k[:, blk]).astype(BF16)
        ka_ref[2 * p + 1] = jnp.where(low, aug_k[:, blk], kn).astype(BF16)
        vt = qkv[:, 2 * D_MODEL + p * LANES:2 * D_MODEL + (p + 1) * LANES].T
        for jj in range(ts // tk):
            vt_ref[p, jj] = vt[:, jj * tk:(jj + 1) * tk].astype(BF16)


def _fox_proj(x, g, w_qkv, w_f, b_f, gq, gk):
    B, S, _ = x.shape
    ts = min(SEQ_TILE, S)
    tk = min(FOX_TK, S)
    consts = _fox_constants()
    kern = functools.partial(_fox_proj_kernel, ts=ts, tk=tk)
    tile = pl.BlockSpec((None, ts, D_MODEL), lambda b, i: (b, i, 0))
    return pl.pallas_call(
        kern,
        out_shape=(jax.ShapeDtypeStruct((B, FOX_HEADS, LANES, S), BF16),
                   jax.ShapeDtypeStruct((B, FOX_HEADS, S, LANES), BF16),
                   jax.ShapeDtypeStruct((B, FOX_HEADS // 2, S // tk, LANES, tk), BF16)),
        grid=(B, S // ts),
        in_specs=[tile, _const_spec((1, D_MODEL)), _const_spec((D_MODEL, 3 * D_MODEL)),
                  _const_spec((D_MODEL, LANES)), _const_spec((1, LANES)),
                  _const_spec((1, D_MODEL)), _const_spec((1, D_MODEL))]
                 + [_const_spec(c.shape) for c in consts],
        out_specs=(pl.BlockSpec((None, FOX_HEADS, LANES, ts), lambda b, i: (b, 0, 0, i)),
                   pl.BlockSpec((None, FOX_HEADS, ts, LANES), lambda b, i: (b, 0, i, 0)),
                   pl.BlockSpec((None, FOX_HEADS // 2, ts // tk, LANES, tk), lambda b, i: (b, 0, i, 0, 0))),
        scratch_shapes=[pltpu.VMEM((1, LANES), F32)],
        compiler_params=_params("parallel", "arbitrary"),
        name="fox_proj",
    )(x, g, w_qkv, w_f, b_f, gq, gk, *consts)


def _fox_attn_kernel(qt_ref, qt_next_ref, k_ref, vt_ref, o_ref, st_ref, *, tq, tqs, tk):
    i = pl.program_id(2)
    nsub = tq // tqs
    nkb = tq // tk
    dh = FOX_HEAD_DIM
    chains = [(h, qs) for h in range(2) for qs in range(nsub)]

    def scores(n, jblk, q_ref=qt_ref):
        h, qs = chains[n]
        koff = pl.multiple_of(jblk * tk, tk)
        return _dot(k_ref[h, pl.ds(koff, tk), :], q_ref[h, :, qs * tqs:(qs + 1) * tqs])

    def softmax_step(st, m):
        m_new = jnp.maximum(m, jnp.max(st, axis=0, keepdims=True))
        return m_new, jnp.exp2(m - m_new), jnp.exp2(st - m_new).astype(BF16)

    ones_rows = jnp.ones((FOX_DEN_ROWS, tk), BF16)

    def values(n, jblk, alpha, p, a):
        h, _ = chains[n]
        vt = jnp.concatenate([vt_ref[jblk, h * dh:(h + 1) * dh, :], ones_rows], axis=0)
        return alpha * a + _dot(vt, p)

    nch = len(chains)

    def block(jblk, slot, states, masks, next_masks):
        cur = slot * nch
        nxt = nch - cur
        out = list(states)

        def issue_next(n):
            if n >= nch:
                return
            if next_masks is None:
                st_ref[nxt + n] = scores(n, 0, qt_next_ref)
            elif next_masks[n] is not False:
                st_ref[nxt + n] = scores(n, jblk + 1)

        issue_next(0)
        for n in range(nch):
            issue_next(n + 1)
            if masks[n] is False:
                continue
            st = st_ref[cur + n]
            if masks[n] is not None:
                st = jnp.where(masks[n], st, NEG_BIG)
            m_new, alpha, p = softmax_step(st, states[n][0])
            out[n] = (m_new, values(n, jblk, alpha, p, states[n][1]))
        return tuple(out)

    row = lax.broadcasted_iota(jnp.int32, (tk, tqs), 0)
    col = lax.broadcasted_iota(jnp.int32, (tk, tqs), 1)
    diag_masks = []
    for kb in range(nkb):
        masks = []
        for _, qs in chains:
            k_lo, q_lo = kb * tk, qs * tqs
            if k_lo > q_lo + tqs - 1:
                masks.append(False)
            else:
                masks.append(None if k_lo + tk - 1 <= q_lo else (row + k_lo <= col + q_lo))
        diag_masks.append(masks)
    no_mask = [None] * len(chains)

    states = tuple((jnp.full((1, tqs), NEG_BIG, F32), jnp.zeros((dh + FOX_DEN_ROWS, tqs), F32))
                   for _ in chains)

    @pl.when(i == 0)
    def _():
        for n in range(nch):
            st_ref[n] = scores(n, 0)

    def one_trip(jj, c):
        for u in range(FOX_BLOCKS_PER_TRIP):
            c = block(FOX_BLOCKS_PER_TRIP * jj + u, u % 2, c, no_mask, no_mask)
        return c

    states = lax.fori_loop(0, i * (nkb // FOX_BLOCKS_PER_TRIP), one_trip, states)
    for kb in range(nkb):
        states = block(i * nkb + kb, kb % 2, states, diag_masks[kb], diag_masks[kb + 1] if kb + 1 < nkb else None)

    def normalized(n):
        acc = states[n][1]
        return acc[:dh] / acc[dh:dh + 1]

    out_t = jnp.concatenate(
        [jnp.concatenate([normalized(h * nsub + qs) for qs in range(nsub)], axis=1) for h in range(2)],
        axis=0)
    o_ref[...] = out_t.T.astype(BF16)


def _fox_attn(qt, ka, vt):
    B, H, _, S = qt.shape
    tq, tqs, tk = min(FOX_TQ, S), min(FOX_TQS, S), min(FOX_TK, S)
    kern = functools.partial(_fox_attn_kernel, tq=tq, tqs=tqs, tk=tk)
    assert S % tq == 0 and tq % tqs == 0 and (tq // tk) % FOX_BLOCKS_PER_TRIP == 0
    last = S // tq - 1
    return pl.pallas_call(
        kern,
        out_shape=jax.ShapeDtypeStruct((B, S, D_MODEL), BF16),
        grid=(B, H // 2, S // tq),
        in_specs=[
            pl.BlockSpec((None, 2, LANES, tq), lambda b, p, i: (b, p, 0, i)),
            pl.BlockSpec((None, 2, LANES, tq), lambda b, p, i: (b, p, 0, jnp.minimum(i + 1, last))),
            pl.BlockSpec((None, 2, S, LANES), lambda b, p, i: (b, p, 0, 0)),
            pl.BlockSpec((None, None, S // tk, LANES, tk), lambda b, p, i: (b, p, 0, 0, 0)),
        ],
        out_specs=pl.BlockSpec((None, tq, LANES), lambda b, p, i: (b, i, p)),
        scratch_shapes=[pltpu.VMEM((2 * 2 * (tq // tqs), tk, tqs), F32)],
        compiler_params=_params("parallel", "parallel", "arbitrary"),
        name="fox_attn",
    )(qt, qt, ka, vt)


GLA_SCALE = GLA_DK_HEAD ** -0.5
GLA_TILE = 512


def _gla_kernel(x_ref, g_ref, win_ref, wg1_ref, wg2_ref, bg_ref, og_ref, y_ref, st_ref, *, ts, nb):
    @pl.when(pl.program_id(0) == 0)
    def _():
        st_ref[...] = jnp.zeros_like(st_ref)

    C = GLA_CHUNK
    hn = _rms(x_ref[...].reshape(nb * ts, D_MODEL), g_ref[...]).astype(BF16)
    tri = (lax.broadcasted_iota(jnp.int32, (C, C), 0)
           >= lax.broadcasted_iota(jnp.int32, (C, C), 1))
    tri3 = jnp.concatenate([tri.astype(BF16)] * N_SPLIT, axis=1)
    og = og_ref[...]
    dks = [slice(h * GLA_DK_HEAD, (h + 1) * GLA_DK_HEAD) for h in range(GLA_HEADS)]
    dvs = [slice(h * GLA_DV_HEAD, (h + 1) * GLA_DV_HEAD) for h in range(GLA_HEADS)]
    items = [(bi, h) for bi in range(nb) for h in range(GLA_HEADS)]
    proj = _dot(hn, win_ref[...])
    gate = _dot(_dot(hn, wg1_ref[...]).astype(BF16), wg2_ref[...]) + bg_ref[...]
    log_a = _log_sigmoid(gate) / GLA_TAU
    for c in range(ts // C):
        rows = [slice(bi * ts + c * C, bi * ts + (c + 1) * C) for bi in range(nb)]
        q_t, k_t, k_dec, decay = [], [], [], []
        for bi in range(nb):
            b = _dot(tri3, jnp.concatenate(_split3(log_a[rows[bi]]), axis=0))
            b_last = b[C - 1:C, :]
            kc = proj[rows[bi], GLA_DK:2 * GLA_DK]
            q_t.append(proj[rows[bi], :GLA_DK] * GLA_SCALE * jnp.exp(b))
            k_t.append(kc * jnp.exp(-b))
            k_dec.append(kc * jnp.exp(b_last - b))
            decay.append(jnp.exp(b_last))
        qs = [q_t[bi][:, dks[h]].astype(BF16) for bi, h in items]
        vs = [proj[rows[bi], 2 * GLA_DK + h * GLA_DV_HEAD:2 * GLA_DK + (h + 1) * GLA_DV_HEAD].astype(BF16)
              for bi, h in items]
        atts = [lax.dot_general(qs[n], k_t[bi][:, dks[h]].astype(BF16), _NT, preferred_element_type=F32)
                for n, (bi, h) in enumerate(items)]
        sts = [st_ref[bi, h] for bi, h in items]
        inter = [lax.dot_general(qs[n], sts[n].astype(BF16), _NT, preferred_element_type=F32)
                 for n in range(len(items))]
        for n, (bi, h) in enumerate(items):
            st_ref[bi, h] = sts[n] * decay[bi][:, dks[h]] + lax.dot_general(
                vs[n], k_dec[bi][:, dks[h]].astype(BF16), _TN, preferred_element_type=F32)
        for n, (bi, h) in enumerate(items):
            o = _dot(jnp.where(tri, atts[n], 0.0).astype(BF16), vs[n]) + inter[n]
            on = o * lax.rsqrt(jnp.mean(o * o, axis=-1, keepdims=True) + RMS_EPS) * og
            r = proj[rows[bi], 2 * GLA_DK + GLA_DV + h * GLA_DV_HEAD:2 * GLA_DK + GLA_DV + (h + 1) * GLA_DV_HEAD]
            y_ref[bi, c * C:(c + 1) * C, dvs[h]] = (on * (r * _sigmoid(r))).astype(BF16)


def _gla(x, g, w_in, w_g1, w_g2, b_g, o_g):
    B, S, _ = x.shape
    ts = min(GLA_TILE, S)
    kern = functools.partial(_gla_kernel, ts=ts, nb=B)
    tile = pl.BlockSpec((B, ts, D_MODEL), lambda i: (0, i, 0))
    return pl.pallas_call(
        kern,
        out_shape=jax.ShapeDtypeStruct((B, S, D_MODEL), BF16),
        grid=(S // ts,),
        in_specs=[tile, _const_spec((1, D_MODEL)), _const_spec((D_MODEL, 2 * GLA_DK + 2 * GLA_DV)),
                  _const_spec((D_MODEL, LANES)), _const_spec((LANES, GLA_DK)),
                  _const_spec((1, GLA_DK)), _const_spec((1, GLA_DV_HEAD))],
        out_specs=tile,
        scratch_shapes=[pltpu.VMEM((B, GLA_HEADS, GLA_DV_HEAD, GLA_DK_HEAD), F32)],
        compiler_params=_params("arbitrary"),
        name="gla",
    )(x, g, w_in, w_g1, w_g2, b_g, o_g)


def _row(v):
    return v.reshape(1, -1).astype(F32)


def _pad_cols(w, n):
    return jnp.pad(w, ((0, 0), (0, n - w.shape[1])))


def kernel(x, mix_norm, mlp_norm, mlp_w1, mlp_w2,
           a_w_in, a_b_in, a_conv_w, a_conv_b, a_ln_g, a_ln_b, a_w_out, a_b_out,
           b_w_in, b_conv_w, b_w_out,
           c_w_qkv, c_q_norm, c_k_norm, c_w_f, c_b_f, c_w_out,
           d_w_in, d_w_g1, d_w_g2, d_b_g, d_o_norm, d_w_out):
    B, S, _ = x.shape
    depth = mix_norm.shape[0]
    zero_bias = jnp.zeros((1, D_MODEL), F32)
    w1_all, w2_all = mlp_w1.astype(BF16), mlp_w2.astype(BF16)
    for i in range(depth):
        m, j = i % 4, i // 4
        g = _row(mix_norm[i])
        if m == 0:
            y = _conformer(x, g, a_w_in[j].astype(BF16), _row(a_b_in[j]), a_conv_w[j], _row(a_conv_b[j]),
                           _row(a_ln_g[j]), _row(a_ln_b[j]))
            wo, bo = a_w_out[j], _row(a_b_out[j])
        elif m == 1:
            y = _short_conv(x, g, b_w_in[j].astype(BF16), b_conv_w[j])
            wo, bo = b_w_out[j], zero_bias
        elif m == 2:
            gq = _row(jnp.tile(c_q_norm[j], FOX_HEADS)) * (FOX_SCALE * LOG2E)
            gk = _row(jnp.tile(c_k_norm[j], FOX_HEADS))
            qt, ka, vt = _fox_proj(x, g, c_w_qkv[j].astype(BF16), _pad_cols(c_w_f[j], LANES).astype(BF16),
                                  _pad_cols(_row(c_b_f[j]), LANES), gq, gk)
            y = _fox_attn(qt, ka, vt)
            wo, bo = c_w_out[j], zero_bias
        else:
            w_g2 = jnp.pad(d_w_g2[j], ((0, LANES - GLA_GATE_RANK), (0, 0)))
            y = _gla(x, g, d_w_in[j].astype(BF16), _pad_cols(d_w_g1[j], LANES).astype(BF16),
                     w_g2.astype(BF16), _row(d_b_g[j]), _row(d_o_norm[j]))
            wo, bo = d_w_out[j], zero_bias
        x = _proj_mlp(x.reshape(B * S, D_MODEL), y.reshape(B * S, D_MODEL), wo.astype(BF16), bo,
                      _row(mlp_norm[i]), w1_all, w2_all, i).reshape(B, S, D_MODEL)
    return x
```

```python
import functools

import numpy as np
import jax
import jax.numpy as jnp
from jax import lax
from jax.experimental import pallas as pl
from jax.experimental.pallas import tpu as pltpu

F32 = jnp.float32
BF16 = jnp.bfloat16

D_MODEL = 1024
D_FF = 4 * D_MODEL
RMS_EPS = 1e-6
CONV_KERNEL = 31
SHORT_CONV = 3
FOX_HEADS = 16
FOX_HEAD_DIM = D_MODEL // FOX_HEADS
GLA_HEADS = 4
GLA_DK = D_MODEL // 2
GLA_DV = D_MODEL
GLA_DK_HEAD = GLA_DK // GLA_HEADS
GLA_DV_HEAD = GLA_DV // GLA_HEADS
GLA_GATE_RANK = 16
GLA_TAU = 16.0
GLA_CHUNK = 64

LANES = 128
SUBLANES = 8
CONV_ROWS = 64
CONV_HALO = 32
SHORT_HALO = 8
NEG_BIG = -1e30
VMEM_LIMIT = 56 * 1024 * 1024

_NT = (((1,), (1,)), ((), ()))
_TN = (((0,), (0,)), ((), ()))


def _dot(a, b):
    return jnp.dot(a, b, preferred_element_type=F32)


def _rms(x, g):
    return x * lax.rsqrt(jnp.mean(x * x, axis=-1, keepdims=True) + RMS_EPS) * g


def _sigmoid(x):
    return 0.5 * jnp.tanh(0.5 * x) + 0.5


def _log_sigmoid(x):
    return jnp.minimum(x, 0.0) - jnp.log1p(jnp.exp(-jnp.abs(x)))


def _split3(x):
    h1 = x.astype(BF16)
    r1 = x - h1.astype(F32)
    h2 = r1.astype(BF16)
    h3 = (r1 - h2.astype(F32)).astype(BF16)
    return h1, h2, h3


def _const_spec(shape):
    nd = len(shape)
    return pl.BlockSpec(shape, lambda *_: (0,) * nd, pipeline_mode=pl.Buffered(1))


def _params(*sem):
    return pltpu.CompilerParams(dimension_semantics=sem, vmem_limit_bytes=VMEM_LIMIT)


MLP_TM = 1024
MLP_TF = 1024


def _mlp_kernel(x_ref, y_ref, wo_ref, bo_ref, g_ref, w1_ref, w2_ref, o_ref):
    x1 = x_ref[...] + _dot(y_ref[...], wo_ref[...]) + bo_ref[...]
    hn = _rms(x1, g_ref[...]).astype(BF16)
    acc = x1
    for c in range(D_FF // MLP_TF):
        cols = slice(c * MLP_TF, (c + 1) * MLP_TF)
        h = jnp.maximum(_dot(hn, w1_ref[:, cols]), 0.0)
        acc = acc + _dot((h * h).astype(BF16), w2_ref[cols, :])
    o_ref[...] = acc


def _layer_spec(shape, layer):
    nd = len(shape)
    return pl.BlockSpec((None,) + tuple(shape), lambda *_: (layer,) + (0,) * nd, pipeline_mode=pl.Buffered(1))


def _proj_mlp(x2d, y2d, wo, bo, g, w1_all, w2_all, layer):
    T = x2d.shape[0]
    tm = min(MLP_TM, T)
    row = lambda i: (i, 0)
    return pl.pallas_call(
        _mlp_kernel,
        out_shape=jax.ShapeDtypeStruct((T, D_MODEL), F32),
        grid=(T // tm,),
        in_specs=[
            pl.BlockSpec((tm, D_MODEL), row),
            pl.BlockSpec((tm, D_MODEL), row),
            _const_spec((D_MODEL, D_MODEL)),
            _const_spec((1, D_MODEL)),
            _const_spec((1, D_MODEL)),
            _layer_spec((D_MODEL, D_FF), layer),
            _layer_spec((D_FF, D_MODEL), layer),
        ],
        out_specs=pl.BlockSpec((tm, D_MODEL), row),
        compiler_params=_params("parallel"),
        name="proj_mlp",
    )(x2d, y2d, wo, bo, g, w1_all, w2_all)


SEQ_TILE = 512


def _conformer_kernel(xh_ref, x_ref, g_ref, win_ref, bin_ref, cw_ref, cb_ref, lg_ref, lb_ref,
                      y_ref, glu_ref, out_ref, *, ts):
    i = pl.program_id(1)
    xa = jnp.concatenate([xh_ref[...], x_ref[...]], axis=0)
    hn = _rms(xa, g_ref[...]).astype(BF16)
    p = _dot(hn, win_ref[...]) + bin_ref[...]
    glu = p[:, :D_MODEL] * _sigmoid(p[:, D_MODEL:])
    rows = lax.broadcasted_iota(jnp.int32, (CONV_HALO + ts, 1), 0)
    glu = jnp.where((rows >= CONV_HALO) | (i > 0), glu, 0.0)
    nlb = D_MODEL // LANES
    for lb in range(nlb):
        glu_ref[lb] = glu[:, lb * LANES:(lb + 1) * LANES]
    first = CONV_HALO - (CONV_KERNEL - 1)

    def lane_block(lb, carry):
        bias = cb_ref[lb]
        w = [cw_ref[lb, k] for k in range(CONV_KERNEL)]

        def row_chunk(rc, carry):
            r0 = pl.multiple_of(rc * CONV_ROWS, CONV_ROWS)
            accs = [bias] * SUBLANES
            for s in range(CONV_KERNEL + SUBLANES - 1):
                rows_s = glu_ref[lb, pl.ds(r0 + first + s, SUBLANES, stride=SUBLANES), :]
                for m in range(SUBLANES):
                    if 0 <= s - m < CONV_KERNEL:
                        accs[m] = accs[m] + w[s - m] * rows_s
            for m in range(SUBLANES):
                out_ref[lb, pl.ds(r0 + m, SUBLANES, stride=SUBLANES), :] = accs[m]
            return carry

        return lax.fori_loop(0, ts // CONV_ROWS, row_chunk, carry, unroll=2)

    lax.fori_loop(0, nlb, lane_block, 0)
    acc = jnp.concatenate([out_ref[lb] for lb in range(nlb)], axis=1)
    xc = acc - jnp.mean(acc, axis=-1, keepdims=True)
    y = xc * lax.rsqrt(jnp.mean(xc * xc, axis=-1, keepdims=True) + RMS_EPS)
    y = y * lg_ref[...] + lb_ref[...]
    y_ref[...] = (y * _sigmoid(y)).astype(BF16)


def _conformer(x, g, w_in, b_in, conv_w, conv_b, ln_g, ln_b):
    B, S, _ = x.shape
    ts = min(SEQ_TILE, S)
    per = ts // CONV_HALO
    kern = functools.partial(_conformer_kernel, ts=ts)
    nlb = D_MODEL // LANES
    conv_w = jnp.broadcast_to(conv_w.reshape(CONV_KERNEL, nlb, 1, LANES).transpose(1, 0, 2, 3),
                              (nlb, CONV_KERNEL, SUBLANES, LANES))
    conv_b = jnp.broadcast_to(conv_b.reshape(nlb, 1, LANES), (nlb, SUBLANES, LANES))
    return pl.pallas_call(
        kern,
        out_shape=jax.ShapeDtypeStruct((B, S, D_MODEL), BF16),
        grid=(B, S // ts),
        in_specs=[
            pl.BlockSpec((None, CONV_HALO, D_MODEL), lambda b, i: (b, jnp.maximum(i * per - 1, 0), 0)),
            pl.BlockSpec((None, ts, D_MODEL), lambda b, i: (b, i, 0)),
            _const_spec((1, D_MODEL)),
            _const_spec((D_MODEL, 2 * D_MODEL)),
            _const_spec((1, 2 * D_MODEL)),
            _const_spec((nlb, CONV_KERNEL, SUBLANES, LANES)),
            _const_spec((nlb, SUBLANES, LANES)),
            _const_spec((1, D_MODEL)),
            _const_spec((1, D_MODEL)),
        ],
        out_specs=pl.BlockSpec((None, ts, D_MODEL), lambda b, i: (b, i, 0)),
        scratch_shapes=[pltpu.VMEM((D_MODEL // LANES, CONV_HALO + ts, LANES), F32),
                        pltpu.VMEM((D_MODEL // LANES, ts, LANES), F32)],
        compiler_params=_params("parallel", "parallel"),
        name="conformer",
    )(x, x, g, w_in, b_in, conv_w, conv_b, ln_g, ln_b)


def _short_conv_kernel(xh_ref, x_ref, g_ref, win_ref, cw_ref, y_ref, u_ref, *, ts):
    i = pl.program_id(1)
    xa = jnp.concatenate([xh_ref[...], x_ref[...]], axis=0)
    hn = _rms(xa, g_ref[...]).astype(BF16)
    p = _dot(hn, win_ref[...])
    u = p[:, D_MODEL:2 * D_MODEL] * p[:, 2 * D_MODEL:]
    rows = lax.broadcasted_iota(jnp.int32, (SHORT_HALO + ts, 1), 0)
    u_ref[...] = jnp.where((rows >= SHORT_HALO) | (i > 0), u, 0.0)
    first = SHORT_HALO - (SHORT_CONV - 1)
    conv = cw_ref[0:1, :] * u_ref[pl.ds(first, ts), :]
    for k in range(1, SHORT_CONV):
        conv = conv + cw_ref[k:k + 1, :] * u_ref[pl.ds(first + k, ts), :]
    y_ref[...] = (p[SHORT_HALO:, :D_MODEL] * conv).astype(BF16)


def _short_conv(x, g, w_in, conv_w):
    B, S, _ = x.shape
    ts = min(SEQ_TILE, S)
    per = ts // SHORT_HALO
    kern = functools.partial(_short_conv_kernel, ts=ts)
    return pl.pallas_call(
        kern,
        out_shape=jax.ShapeDtypeStruct((B, S, D_MODEL), BF16),
        grid=(B, S // ts),
        in_specs=[
            pl.BlockSpec((None, SHORT_HALO, D_MODEL), lambda b, i: (b, jnp.maximum(i * per - 1, 0), 0)),
            pl.BlockSpec((None, ts, D_MODEL), lambda b, i: (b, i, 0)),
            _const_spec((1, D_MODEL)),
            _const_spec((D_MODEL, 3 * D_MODEL)),
            _const_spec((SHORT_CONV, D_MODEL)),
        ],
        out_specs=pl.BlockSpec((None, ts, D_MODEL), lambda b, i: (b, i, 0)),
        scratch_shapes=[pltpu.VMEM((SHORT_HALO + ts, D_MODEL), F32)],
        compiler_params=_params("parallel", "parallel"),
        name="short_conv",
    )(x, x, g, w_in, conv_w)


FOX_SCALE = FOX_HEAD_DIM ** -0.5
LOG2E = 1.4426950408889634
N_SPLIT = 3
FOX_TQ = 2048
FOX_BLOCKS_PER_TRIP = 8
FOX_DEN_ROWS = 16
FOX_TQS = 256
FOX_TK = 256


def _fox_constants():
    H, dh = FOX_HEADS, FOX_HEAD_DIM
    p_q = np.zeros((LANES, D_MODEL), np.float32)
    p_k = np.zeros((LANES, D_MODEL), np.float32)
    one_q = np.zeros((1, D_MODEL), np.float32)
    one_k = np.zeros((1, D_MODEL), np.float32)
    for h in range(H):
        base = (h // 2) * LANES + (dh if h % 2 == 0 else 0)
        for s in range(N_SPLIT):
            p_q[s * H + h, base + s] = 1.0
            one_q[0, base + N_SPLIT + s] = 1.0
            one_k[0, base + s] = 1.0
            p_k[s * H + h, base + N_SPLIT + s] = -1.0
    return jnp.asarray(p_q, BF16), jnp.asarray(p_k, BF16), jnp.asarray(one_q), jnp.asarray(one_k)


def _pack_pieces(pieces):
    out = pieces[0].astype(F32)
    for s in range(1, N_SPLIT):
        out = out + pltpu.roll(pieces[s].astype(F32), s * FOX_HEADS, 1)
    return out.astype(BF16)


def _pair_rms(blk, gain, low):
    sq = blk * blk
    s_low = jnp.sum(jnp.where(low, sq, 0.0), axis=-1, keepdims=True)
    s_high = jnp.sum(jnp.where(low, 0.0, sq), axis=-1, keepdims=True)
    inv = jnp.where(low, lax.rsqrt(s_low * (1.0 / FOX_HEAD_DIM) + RMS_EPS),
                    lax.rsqrt(s_high * (1.0 / FOX_HEAD_DIM) + RMS_EPS))
    return blk * inv * gain


def _fox_proj_kernel(x_ref, g_ref, wqkv_ref, wf_ref, bf_ref, gq_ref, gk_ref,
                     pq_ref, pk_ref, oneq_ref, onek_ref, qt_ref, ka_ref, vt_ref, carry_ref, *, ts, tk):
    @pl.when(pl.program_id(1) == 0)
    def _():
        carry_ref[...] = jnp.zeros_like(carry_ref)

    hn = _rms(x_ref[...], g_ref[...]).astype(BF16)
    qkv = _dot(hn, wqkv_ref[...])

    lane = lax.broadcasted_iota(jnp.int32, (1, LANES), 1)
    logf = _log_sigmoid(_dot(hn, wf_ref[...]) + bf_ref[...])
    logf = jnp.where(lane < FOX_HEADS, logf, 0.0)
    tri = (lax.broadcasted_iota(jnp.int32, (ts, ts), 0)
           >= lax.broadcasted_iota(jnp.int32, (ts, ts), 1)).astype(BF16)
    cs = _dot(tri, _pack_pieces(_split3(logf)))
    c = cs + pltpu.roll(cs, LANES - FOX_HEADS, 1) + pltpu.roll(cs, LANES - 2 * FOX_HEADS, 1)
    c = jnp.where(lane < FOX_HEADS, c, 0.0) + carry_ref[...]
    carry_ref[...] = c[ts - 1:ts, :]
    packed = _pack_pieces(_split3(c * LOG2E))
    aug_q = _dot(packed, pq_ref[...]) + oneq_ref[...]
    aug_k = _dot(packed, pk_ref[...]) + onek_ref[...]

    low = lane < FOX_HEAD_DIM
    for p in range(FOX_HEADS // 2):
        blk = slice(p * LANES, (p + 1) * LANES)
        qn = _pair_rms(qkv[:, blk], gq_ref[:, blk], low)
        kn = _pair_rms(qkv[:, D_MODEL + p * LANES:D_MODEL + (p + 1) * LANES], gk_ref[:, blk], low)
        qt_ref[2 * p] = jnp.where(low, qn, aug_q[:, blk]).T.astype(BF16)
        qt_ref[2 * p + 1] = jnp.where(low, aug_q[:, blk], qn).T.astype(BF16)
        ka_ref[2 * p] = jnp.where(low, kn, aug_k[:, blk]).astype(BF16)
        ka_ref[2 * p + 1] = jnp.where(low, aug_k[:, blk], kn).astype(BF16)
        vt = qkv[:, 2 * D_MODEL + p * LANES:2 * D_MODEL + (p + 1) * LANES].T
        for jj in range(ts // tk):
            vt_ref[p, jj] = vt[:, jj * tk:(jj + 1) * tk].astype(BF16)


def _fox_proj(x, g, w_qkv, w_f, b_f, gq, gk):
    B, S, _ = x.shape
    ts = min(SEQ_TILE, S)
    tk = min(FOX_TK, S)
    consts = _fox_constants()
    kern = functools.partial(_fox_proj_kernel, ts=ts, tk=tk)
    tile = pl.BlockSpec((None, ts, D_MODEL), lambda b, i: (b, i, 0))
    return pl.pallas_call(
        kern,
        out_shape=(jax.ShapeDtypeStruct((B, FOX_HEADS, LANES, S), BF16),
                   jax.ShapeDtypeStruct((B, FOX_HEADS, S, LANES), BF16),
                   jax.ShapeDtypeStruct((B, FOX_HEADS // 2, S // tk, LANES, tk), BF16)),
        grid=(B, S // ts),
        in_specs=[tile, _const_spec((1, D_MODEL)), _const_spec((D_MODEL, 3 * D_MODEL)),
                  _const_spec((D_MODEL, LANES)), _const_spec((1, LANES)),
                  _const_spec((1, D_MODEL)), _const_spec((1, D_MODEL))]
                 + [_const_spec(c.shape) for c in consts],
        out_specs=(pl.BlockSpec((None, FOX_HEADS, LANES, ts), lambda b, i: (b, 0, 0, i)),
                   pl.BlockSpec((None, FOX_HEADS, ts, LANES), lambda b, i: (b, 0, i, 0)),
                   pl.BlockSpec((None, FOX_HEADS // 2, ts // tk, LANES, tk), lambda b, i: (b, 0, i, 0, 0))),
        scratch_shapes=[pltpu.VMEM((1, LANES), F32)],
        compiler_params=_params("parallel", "arbitrary"),
        name="fox_proj",
    )(x, g, w_qkv, w_f, b_f, gq, gk, *consts)


def _fox_attn_kernel(qt_ref, qt_next_ref, k_ref, vt_ref, o_ref, st_ref, *, tq, tqs, tk):
    i = pl.program_id(2)
    nsub = tq // tqs
    nkb = tq // tk
    dh = FOX_HEAD_DIM
    chains = [(h, qs) for h in range(2) for qs in range(nsub)]

    def scores(n, jblk, q_ref=qt_ref):
        h, qs = chains[n]
        koff = pl.multiple_of(jblk * tk, tk)
        return _dot(k_ref[h, pl.ds(koff, tk), :], q_ref[h, :, qs * tqs:(qs + 1) * tqs])

    def softmax_step(st, m):
        m_new = jnp.maximum(m, jnp.max(st, axis=0, keepdims=True))
        return m_new, jnp.exp2(m - m_new), jnp.exp2(st - m_new).astype(BF16)

    ones_rows = jnp.ones((FOX_DEN_ROWS, tk), BF16)

    def values(n, jblk, alpha, p, a):
        h, _ = chains[n]
        vt = jnp.concatenate([vt_ref[jblk, h * dh:(h + 1) * dh, :], ones_rows], axis=0)
        return alpha * a + _dot(vt, p)

    nch = len(chains)

    def block(jblk, slot, states, masks, next_masks):
        cur = slot * nch
        nxt = nch - cur
        out = list(states)

        def issue_next(n):
            if n >= nch:
                return
            if next_masks is None:
                st_ref[nxt + n] = scores(n, 0, qt_next_ref)
            elif next_masks[n] is not False:
                st_ref[nxt + n] = scores(n, jblk + 1)

        issue_next(0)
        for n in range(nch):
            issue_next(n + 1)
            if masks[n] is False:
                continue
            st = st_ref[cur + n]
            if masks[n] is not None:
                st = jnp.where(masks[n], st, NEG_BIG)
            m_new, alpha, p = softmax_step(st, states[n][0])
            out[n] = (m_new, values(n, jblk, alpha, p, states[n][1]))
        return tuple(out)

    row = lax.broadcasted_iota(jnp.int32, (tk, tqs), 0)
    col = lax.broadcasted_iota(jnp.int32, (tk, tqs), 1)
    diag_masks = []
    for kb in range(nkb):
        masks = []
        for _, qs in chains:
            k_lo, q_lo = kb * tk, qs * tqs
            if k_lo > q_lo + tqs - 1:
                masks.append(False)
            else:
                masks.append(None if k_lo + tk - 1 <= q_lo else (row + k_lo <= col + q_lo))
        diag_masks.append(masks)
    no_mask = [None] * len(chains)

    states = tuple((jnp.full((1, tqs), NEG_BIG, F32), jnp.zeros((dh + FOX_DEN_ROWS, tqs), F32))
                   for _ in chains)

    @pl.when(i == 0)
    def _():
        for n in range(nch):
            st_ref[n] = scores(n, 0)

    def one_trip(jj, c):
        for u in range(FOX_BLOCKS_PER_TRIP):
            c = block(FOX_BLOCKS_PER_TRIP * jj + u, u % 2, c, no_mask, no_mask)
        return c

    states = lax.fori_loop(0, i * (nkb // FOX_BLOCKS_PER_TRIP), one_trip, states)
    for kb in range(nkb):
        states = block(i * nkb + kb, kb % 2, states, diag_masks[kb], diag_masks[kb + 1] if kb + 1 < nkb else None)

    def normalized(n):
        acc = states[n][1]
        return acc[:dh] / acc[dh:dh + 1]

    out_t = jnp.concatenate(
        [jnp.concatenate([normalized(h * nsub + qs) for qs in range(nsub)], axis=1) for h in range(2)],
        axis=0)
    o_ref[...] = out_t.T.astype(BF16)


def _fox_attn(qt, ka, vt):
    B, H, _, S = qt.shape
    tq, tqs, tk = min(FOX_TQ, S), min(FOX_TQS, S), min(FOX_TK, S)
    kern = functools.partial(_fox_attn_kernel, tq=tq, tqs=tqs, tk=tk)
    assert S % tq == 0 and tq % tqs == 0 and (tq // tk) % FOX_BLOCKS_PER_TRIP == 0
    last = S // tq - 1
    return pl.pallas_call(
        kern,
        out_shape=jax.ShapeDtypeStruct((B, S, D_MODEL), BF16),
        grid=(B, H // 2, S // tq),
        in_specs=[
            pl.BlockSpec((None, 2, LANES, tq), lambda b, p, i: (b, p, 0, i)),
            pl.BlockSpec((None, 2, LANES, tq), lambda b, p, i: (b, p, 0, jnp.minimum(i + 1, last))),
            pl.BlockSpec((None, 2, S, LANES), lambda b, p, i: (b, p, 0, 0)),
            pl.BlockSpec((None, None, S // tk, LANES, tk), lambda b, p, i: (b, p, 0, 0, 0)),
        ],
        out_specs=pl.BlockSpec((None, tq, LANES), lambda b, p, i: (b, i, p)),
        scratch_shapes=[pltpu.VMEM((2 * 2 * (tq // tqs), tk, tqs), F32)],
        compiler_params=_params("parallel", "parallel", "arbitrary"),
        name="fox_attn",
    )(qt, qt, ka, vt)


GLA_SCALE = GLA_DK_HEAD ** -0.5
GLA_TILE = 512


def _gla_kernel(x_ref, g_ref, win_ref, wg1_ref, wg2_ref, bg_ref, og_ref, y_ref, st_ref, *, ts, nb):
    @pl.when(pl.program_id(0) == 0)
    def _():
        st_ref[...] = jnp.zeros_like(st_ref)

    C = GLA_CHUNK
    hn = _rms(x_ref[...].reshape(nb * ts, D_MODEL), g_ref[...]).astype(BF16)
    tri = (lax.broadcasted_iota(jnp.int32, (C, C), 0)
           >= lax.broadcasted_iota(jnp.int32, (C, C), 1))
    tri3 = jnp.concatenate([tri.astype(BF16)] * N_SPLIT, axis=1)
    og = og_ref[...]
    dks = [slice(h * GLA_DK_HEAD, (h + 1) * GLA_DK_HEAD) for h in range(GLA_HEADS)]
    dvs = [slice(h * GLA_DV_HEAD, (h + 1) * GLA_DV_HEAD) for h in range(GLA_HEADS)]
    items = [(bi, h) for bi in range(nb) for h in range(GLA_HEADS)]
    proj = _dot(hn, win_ref[...])
    gate = _dot(_dot(hn, wg1_ref[...]).astype(BF16), wg2_ref[...]) + bg_ref[...]
    log_a = _log_sigmoid(gate) / GLA_TAU
    for c in range(ts // C):
        rows = [slice(bi * ts + c * C, bi * ts + (c + 1) * C) for bi in range(nb)]
        q_t, k_t, k_dec, decay = [], [], [], []
        for bi in range(nb):
            b = _dot(tri3, jnp.concatenate(_split3(log_a[rows[bi]]), axis=0))
            b_last = b[C - 1:C, :]
            kc = proj[rows[bi], GLA_DK:2 * GLA_DK]
            q_t.append(proj[rows[bi], :GLA_DK] * GLA_SCALE * jnp.exp(b))
            k_t.append(kc * jnp.exp(-b))
            k_dec.append(kc * jnp.exp(b_last - b))
            decay.append(jnp.exp(b_last))
        qs = [q_t[bi][:, dks[h]].astype(BF16) for bi, h in items]
        vs = [proj[rows[bi], 2 * GLA_DK + h * GLA_DV_HEAD:2 * GLA_DK + (h + 1) * GLA_DV_HEAD].astype(BF16)
              for bi, h in items]
        atts = [lax.dot_general(qs[n], k_t[bi][:, dks[h]].astype(BF16), _NT, preferred_element_type=F32)
                for n, (bi, h) in enumerate(items)]
        sts = [st_ref[bi, h] for bi, h in items]
        inter = [lax.dot_general(qs[n], sts[n].astype(BF16), _NT, preferred_element_type=F32)
                 for n in range(len(items))]
        for n, (bi, h) in enumerate(items):
            st_ref[bi, h] = sts[n] * decay[bi][:, dks[h]] + lax.dot_general(
                vs[n], k_dec[bi][:, dks[h]].astype(BF16), _TN, preferred_element_type=F32)
        for n, (bi, h) in enumerate(items):
            o = _dot(jnp.where(tri, atts[n], 0.0).astype(BF16), vs[n]) + inter[n]
            on = o * lax.rsqrt(jnp.mean(o * o, axis=-1, keepdims=True) + RMS_EPS) * og
            r = proj[rows[bi], 2 * GLA_DK + GLA_DV + h * GLA_DV_HEAD:2 * GLA_DK + GLA_DV + (h + 1) * GLA_DV_HEAD]
            y_ref[bi, c * C:(c + 1) * C, dvs[h]] = (on * (r * _sigmoid(r))).astype(BF16)


def _gla(x, g, w_in, w_g1, w_g2, b_g, o_g):
    B, S, _ = x.shape
    ts = min(GLA_TILE, S)
    kern = functools.partial(_gla_kernel, ts=ts, nb=B)
    tile = pl.BlockSpec((B, ts, D_MODEL), lambda i: (0, i, 0))
    return pl.pallas_call(
        kern,
        out_shape=jax.ShapeDtypeStruct((B, S, D_MODEL), BF16),
        grid=(S // ts,),
        in_specs=[tile, _const_spec((1, D_MODEL)), _const_spec((D_MODEL, 2 * GLA_DK + 2 * GLA_DV)),
                  _const_spec((D_MODEL, LANES)), _const_spec((LANES, GLA_DK)),
                  _const_spec((1, GLA_DK)), _const_spec((1, GLA_DV_HEAD))],
        out_specs=tile,
        scratch_shapes=[pltpu.VMEM((B, GLA_HEADS, GLA_DV_HEAD, GLA_DK_HEAD), F32)],
        compiler_params=_params("arbitrary"),
        name="gla",
    )(x, g, w_in, w_g1, w_g2, b_g, o_g)


def _row(v):
    return v.reshape(1, -1).astype(F32)


def _pad_cols(w, n):
    return jnp.pad(w, ((0, 0), (0, n - w.shape[1])))


def kernel(x, mix_norm, mlp_norm, mlp_w1, mlp_w2,
           a_w_in, a_b_in, a_conv_w, a_conv_b, a_ln_g, a_ln_b, a_w_out, a_b_out,
           b_w_in, b_conv_w, b_w_out,
           c_w_qkv, c_q_norm, c_k_norm, c_w_f, c_b_f, c_w_out,
           d_w_in, d_w_g1, d_w_g2, d_b_g, d_o_norm, d_w_out):
    B, S, _ = x.shape
    depth = mix_norm.shape[0]
    zero_bias = jnp.zeros((1, D_MODEL), F32)
    w1_all, w2_all = mlp_w1.astype(BF16), mlp_w2.astype(BF16)
    for i in range(depth):
        m, j = i % 4, i // 4
        g = _row(mix_norm[i])
        if m == 0:
            y = _conformer(x, g, a_w_in[j].astype(BF16), _row(a_b_in[j]), a_conv_w[j], _row(a_conv_b[j]),
                           _row(a_ln_g[j]), _row(a_ln_b[j]))
            wo, bo = a_w_out[j], _row(a_b_out[j])
        elif m == 1:
            y = _short_conv(x, g, b_w_in[j].astype(BF16), b_conv_w[j])
            wo, bo = b_w_out[j], zero_bias
        elif m == 2:
            gq = _row(jnp.tile(c_q_norm[j], FOX_HEADS)) * (FOX_SCALE * LOG2E)
            gk = _row(jnp.tile(c_k_norm[j], FOX_HEADS))
            qt, ka, vt = _fox_proj(x, g, c_w_qkv[j].astype(BF16), _pad_cols(c_w_f[j], LANES).astype(BF16),
                                  _pad_cols(_row(c_b_f[j]), LANES), gq, gk)
            y = _fox_attn(qt, ka, vt)
            wo, bo = c_w_out[j], zero_bias
        else:
            w_g2 = jnp.pad(d_w_g2[j], ((0, LANES - GLA_GATE_RANK), (0, 0)))
            y = _gla(x, g, d_w_in[j].astype(BF16), _pad_cols(d_w_g1[j], LANES).astype(BF16),
                     w_g2.astype(BF16), _row(d_b_g[j]), _row(d_o_norm[j]))
            wo, bo = d_w_out[j], zero_bias
        x = _proj_mlp(x.reshape(B * S, D_MODEL), y.reshape(B * S, D_MODEL), wo.astype(BF16), bo,
                      _row(mlp_norm[i]), w1_all, w2_all, i).reshape(B, S, D_MODEL)
    return x
```

```python
import functools

import numpy as np
import jax
import jax.numpy as jnp
from jax import lax
from jax.experimental import pallas as pl
from jax.experimental.pallas import tpu as pltpu

F32 = jnp.float32
BF16 = jnp.bfloat16

D_MODEL = 1024
D_FF = 4 * D_MODEL
RMS_EPS = 1e-6
CONV_KERNEL = 31
SHORT_CONV = 3
FOX_HEADS = 16
FOX_HEAD_DIM = D_MODEL // FOX_HEADS
GLA_HEADS = 4
GLA_DK = D_MODEL // 2
GLA_DV = D_MODEL
GLA_DK_HEAD = GLA_DK // GLA_HEADS
GLA_DV_HEAD = GLA_DV // GLA_HEADS
GLA_GATE_RANK = 16
GLA_TAU = 16.0
GLA_CHUNK = 64

LANES = 128
SUBLANES = 8
CONV_ROWS = 64
CONV_HALO = 32
SHORT_HALO = 8
NEG_BIG = -1e30
VMEM_LIMIT = 56 * 1024 * 1024

_NT = (((1,), (1,)), ((), ()))
_TN = (((0,), (0,)), ((), ()))


def _dot(a, b):
    return jnp.dot(a, b, preferred_element_type=F32)


def _rms(x, g):
    return x * lax.rsqrt(jnp.mean(x * x, axis=-1, keepdims=True) + RMS_EPS) * g


def _sigmoid(x):
    return 0.5 * jnp.tanh(0.5 * x) + 0.5


def _log_sigmoid(x):
    return jnp.minimum(x, 0.0) - jnp.log1p(jnp.exp(-jnp.abs(x)))


def _split3(x):
    h1 = x.astype(BF16)
    r1 = x - h1.astype(F32)
    h2 = r1.astype(BF16)
    h3 = (r1 - h2.astype(F32)).astype(BF16)
    return h1, h2, h3


def _const_spec(shape):
    nd = len(shape)
    return pl.BlockSpec(shape, lambda *_: (0,) * nd, pipeline_mode=pl.Buffered(1))


def _params(*sem):
    return pltpu.CompilerParams(dimension_semantics=sem, vmem_limit_bytes=VMEM_LIMIT)


MLP_TM = 1024
MLP_TF = 1024


def _mlp_kernel(x_ref, y_ref, wo_ref, bo_ref, g_ref, w1_ref, w2_ref, o_ref):
    x1 = x_ref[...] + _dot(y_ref[...], wo_ref[...]) + bo_ref[...]
    hn = _rms(x1, g_ref[...]).astype(BF16)
    acc = x1
    for c in range(D_FF // MLP_TF):
        cols = slice(c * MLP_TF, (c + 1) * MLP_TF)
        h = jnp.maximum(_dot(hn, w1_ref[:, cols]), 0.0)
        acc = acc + _dot((h * h).astype(BF16), w2_ref[cols, :])
    o_ref[...] = acc


def _layer_spec(shape, layer):
    nd = len(shape)
    return pl.BlockSpec((None,) + tuple(shape), lambda *_: (layer,) + (0,) * nd, pipeline_mode=pl.Buffered(1))


def _proj_mlp(x2d, y2d, wo, bo, g, w1_all, w2_all, layer):
    T = x2d.shape[0]
    tm = min(MLP_TM, T)
    row = lambda i: (i, 0)
    return pl.pallas_call(
        _mlp_kernel,
        out_shape=jax.ShapeDtypeStruct((T, D_MODEL), F32),
        grid=(T // tm,),
        in_specs=[
            pl.BlockSpec((tm, D_MODEL), row),
            pl.BlockSpec((tm, D_MODEL), row),
            _const_spec((D_MODEL, D_MODEL)),
            _const_spec((1, D_MODEL)),
            _const_spec((1, D_MODEL)),
            _layer_spec((D_MODEL, D_FF), layer),
            _layer_spec((D_FF, D_MODEL), layer),
        ],
        out_specs=pl.BlockSpec((tm, D_MODEL), row),
        compiler_params=_params("parallel"),
        name="proj_mlp",
    )(x2d, y2d, wo, bo, g, w1_all, w2_all)


SEQ_TILE = 512


def _conformer_kernel(xh_ref, x_ref, g_ref, win_ref, bin_ref, cw_ref, cb_ref, lg_ref, lb_ref,
                      y_ref, glu_ref, out_ref, *, ts):
    i = pl.program_id(1)
    xa = jnp.concatenate([xh_ref[...], x_ref[...]], axis=0)
    hn = _rms(xa, g_ref[...]).astype(BF16)
    p = _dot(hn, win_ref[...]) + bin_ref[...]
    glu = p[:, :D_MODEL] * _sigmoid(p[:, D_MODEL:])
    rows = lax.broadcasted_iota(jnp.int32, (CONV_HALO + ts, 1), 0)
    glu = jnp.where((rows >= CONV_HALO) | (i > 0), glu, 0.0)
    nlb = D_MODEL // LANES
    for lb in range(nlb):
        glu_ref[lb] = glu[:, lb * LANES:(lb + 1) * LANES]
    first = CONV_HALO - (CONV_KERNEL - 1)

    def lane_block(lb, carry):
        bias = cb_ref[lb]
        w = [cw_ref[lb, k] for k in range(CONV_KERNEL)]

        def row_chunk(rc, carry):
            r0 = pl.multiple_of(rc * CONV_ROWS, CONV_ROWS)
            accs = [bias] * SUBLANES
            for s in range(CONV_KERNEL + SUBLANES - 1):
                rows_s = glu_ref[lb, pl.ds(r0 + first + s, SUBLANES, stride=SUBLANES), :]
                for m in range(SUBLANES):
                    if 0 <= s - m < CONV_KERNEL:
                        accs[m] = accs[m] + w[s - m] * rows_s
            for m in range(SUBLANES):
                out_ref[lb, pl.ds(r0 + m, SUBLANES, stride=SUBLANES), :] = accs[m]
            return carry

        return lax.fori_loop(0, ts // CONV_ROWS, row_chunk, carry, unroll=4)

    lax.fori_loop(0, nlb, lane_block, 0)
    acc = jnp.concatenate([out_ref[lb] for lb in range(nlb)], axis=1)
    xc = acc - jnp.mean(acc, axis=-1, keepdims=True)
    y = xc * lax.rsqrt(jnp.mean(xc * xc, axis=-1, keepdims=True) + RMS_EPS)
    y = y * lg_ref[...] + lb_ref[...]
    y_ref[...] = (y * _sigmoid(y)).astype(BF16)


def _conformer(x, g, w_in, b_in, conv_w, conv_b, ln_g, ln_b):
    B, S, _ = x.shape
    ts = min(SEQ_TILE, S)
    per = ts // CONV_HALO
    kern = functools.partial(_conformer_kernel, ts=ts)
    nlb = D_MODEL // LANES
    conv_w = jnp.broadcast_to(conv_w.reshape(CONV_KERNEL, nlb, 1, LANES).transpose(1, 0, 2, 3),
                              (nlb, CONV_KERNEL, SUBLANES, LANES))
    conv_b = jnp.broadcast_to(conv_b.reshape(nlb, 1, LANES), (nlb, SUBLANES, LANES))
    return pl.pallas_call(
        kern,
        out_shape=jax.ShapeDtypeStruct((B, S, D_MODEL), BF16),
        grid=(B, S // ts),
        in_specs=[
            pl.BlockSpec((None, CONV_HALO, D_MODEL), lambda b, i: (b, jnp.maximum(i * per - 1, 0), 0)),
            pl.BlockSpec((None, ts, D_MODEL), lambda b, i: (b, i, 0)),
            _const_spec((1, D_MODEL)),
            _const_spec((D_MODEL, 2 * D_MODEL)),
            _const_spec((1, 2 * D_MODEL)),
            _const_spec((nlb, CONV_KERNEL, SUBLANES, LANES)),
            _const_spec((nlb, SUBLANES, LANES)),
            _const_spec((1, D_MODEL)),
            _const_spec((1, D_MODEL)),
        ],
        out_specs=pl.BlockSpec((None, ts, D_MODEL), lambda b, i: (b, i, 0)),
        scratch_shapes=[pltpu.VMEM((D_MODEL // LANES, CONV_HALO + ts, LANES), F32),
                        pltpu.VMEM((D_MODEL // LANES, ts, LANES), F32)],
        compiler_params=_params("parallel", "parallel"),
        name="conformer",
    )(x, x, g, w_in, b_in, conv_w, conv_b, ln_g, ln_b)


def _short_conv_kernel(xh_ref, x_ref, g_ref, win_ref, cw_ref, y_ref, u_ref, *, ts):
    i = pl.program_id(1)
    xa = jnp.concatenate([xh_ref[...], x_ref[...]], axis=0)
    hn = _rms(xa, g_ref[...]).astype(BF16)
    p = _dot(hn, win_ref[...])
    u = p[:, D_MODEL:2 * D_MODEL] * p[:, 2 * D_MODEL:]
    rows = lax.broadcasted_iota(jnp.int32, (SHORT_HALO + ts, 1), 0)
    u_ref[...] = jnp.where((rows >= SHORT_HALO) | (i > 0), u, 0.0)
    first = SHORT_HALO - (SHORT_CONV - 1)
    conv = cw_ref[0:1, :] * u_ref[pl.ds(first, ts), :]
    for k in range(1, SHORT_CONV):
        conv = conv + cw_ref[k:k + 1, :] * u_ref[pl.ds(first + k, ts), :]
    y_ref[...] = (p[SHORT_HALO:, :D_MODEL] * conv).astype(BF16)


def _short_conv(x, g, w_in, conv_w):
    B, S, _ = x.shape
    ts = min(SEQ_TILE, S)
    per = ts // SHORT_HALO
    kern = functools.partial(_short_conv_kernel, ts=ts)
    return pl.pallas_call(
        kern,
        out_shape=jax.ShapeDtypeStruct((B, S, D_MODEL), BF16),
        grid=(B, S // ts),
        in_specs=[
            pl.BlockSpec((None, SHORT_HALO, D_MODEL), lambda b, i: (b, jnp.maximum(i * per - 1, 0), 0)),
            pl.BlockSpec((None, ts, D_MODEL), lambda b, i: (b, i, 0)),
            _const_spec((1, D_MODEL)),
            _const_spec((D_MODEL, 3 * D_MODEL)),
            _const_spec((SHORT_CONV, D_MODEL)),
        ],
        out_specs=pl.BlockSpec((None, ts, D_MODEL), lambda b, i: (b, i, 0)),
        scratch_shapes=[pltpu.VMEM((SHORT_HALO + ts, D_MODEL), F32)],
        compiler_params=_params("parallel", "parallel"),
        name="short_conv",
    )(x, x, g, w_in, conv_w)


FOX_SCALE = FOX_HEAD_DIM ** -0.5
LOG2E = 1.4426950408889634
N_SPLIT = 3
FOX_TQ = 2048
FOX_BLOCKS_PER_TRIP = 8
FOX_DEN_ROWS = 16
FOX_TQS = 256
FOX_TK = 256


def _fox_constants():
    H, dh = FOX_HEADS, FOX_HEAD_DIM
    p_q = np.zeros((LANES, D_MODEL), np.float32)
    p_k = np.zeros((LANES, D_MODEL), np.float32)
    one_q = np.zeros((1, D_MODEL), np.float32)
    one_k = np.zeros((1, D_MODEL), np.float32)
    for h in range(H):
        base = (h // 2) * LANES + (dh if h % 2 == 0 else 0)
        for s in range(N_SPLIT):
            p_q[s * H + h, base + s] = 1.0
            one_q[0, base + N_SPLIT + s] = 1.0
            one_k[0, base + s] = 1.0
            p_k[s * H + h, base + N_SPLIT + s] = -1.0
    return jnp.asarray(p_q, BF16), jnp.asarray(p_k, BF16), jnp.asarray(one_q), jnp.asarray(one_k)


def _pack_pieces(pieces):
    out = pieces[0].astype(F32)
    for s in range(1, N_SPLIT):
        out = out + pltpu.roll(pieces[s].astype(F32), s * FOX_HEADS, 1)
    return out.astype(BF16)


def _pair_rms(blk, gain, low):
    sq = blk * blk
    s_low = jnp.sum(jnp.where(low, sq, 0.0), axis=-1, keepdims=True)
    s_high = jnp.sum(jnp.where(low, 0.0, sq), axis=-1, keepdims=True)
    inv = jnp.where(low, lax.rsqrt(s_low * (1.0 / FOX_HEAD_DIM) + RMS_EPS),
                    lax.rsqrt(s_high * (1.0 / FOX_HEAD_DIM) + RMS_EPS))
    return blk * inv * gain


def _fox_proj_kernel(x_ref, g_ref, wqkv_ref, wf_ref, bf_ref, gq_ref, gk_ref,
                     pq_ref, pk_ref, oneq_ref, onek_ref, qt_ref, ka_ref, vt_ref, carry_ref, *, ts, tk):
    @pl.when(pl.program_id(1) == 0)
    def _():
        carry_ref[...] = jnp.zeros_like(carry_ref)

    hn = _rms(x_ref[...], g_ref[...]).astype(BF16)
    qkv = _dot(hn, wqkv_ref[...])

    lane = lax.broadcasted_iota(jnp.int32, (1, LANES), 1)
    logf = _log_sigmoid(_dot(hn, wf_ref[...]) + bf_ref[...])
    logf = jnp.where(lane < FOX_HEADS, logf, 0.0)
    tri = (lax.broadcasted_iota(jnp.int32, (ts, ts), 0)
           >= lax.broadcasted_iota(jnp.int32, (ts, ts), 1)).astype(BF16)
    cs = _dot(tri, _pack_pieces(_split3(logf)))
    c = cs + pltpu.roll(cs, LANES - FOX_HEADS, 1) + pltpu.roll(cs, LANES - 2 * FOX_HEADS, 1)
    c = jnp.where(lane < FOX_HEADS, c, 0.0) + carry_ref[...]
    carry_ref[...] = c[ts - 1:ts, :]
    packed = _pack_pieces(_split3(c * LOG2E))
    aug_q = _dot(packed, pq_ref[...]) + oneq_ref[...]
    aug_k = _dot(packed, pk_ref[...]) + onek_ref[...]

    low = lane < FOX_HEAD_DIM
    for p in range(FOX_HEADS // 2):
        blk = slice(p * LANES, (p + 1) * LANES)
        qn = _pair_rms(qkv[:, blk], gq_ref[:, blk], low)
        kn = _pair_rms(qkv[:, D_MODEL + p * LANES:D_MODEL + (p + 1) * LANES], gk_ref[:, blk], low)
        qt_ref[2 * p] = jnp.where(low, qn, aug_q[:, blk]).T.astype(BF16)
        qt_ref[2 * p + 1] = jnp.where(low, aug_q[:, blk], qn).T.astype(BF16)
        ka_ref[2 * p] = jnp.where(low, kn, aug_k[:, blk]).astype(BF16)
        ka_ref[2 * p + 1] = jnp.where(low, aug_k[:, blk], kn).astype(BF16)
        vt = qkv[:, 2 * D_MODEL + p * LANES:2 * D_MODEL + (p + 1) * LANES].T
        for jj in range(ts // tk):
            vt_ref[p, jj] = vt[:, jj * tk:(jj + 1) * tk].astype(BF16)


def _fox_proj(x, g, w_qkv, w_f, b_f, gq, gk):
    B, S, _ = x.shape
    ts = min(SEQ_TILE, S)
    tk = min(FOX_TK, S)
    consts = _fox_constants()
    kern = functools.partial(_fox_proj_kernel, ts=ts, tk=tk)
    tile = pl.BlockSpec((None, ts, D_MODEL), lambda b, i: (b, i, 0))
    return pl.pallas_call(
        kern,
        out_shape=(jax.ShapeDtypeStruct((B, FOX_HEADS, LANES, S), BF16),
                   jax.ShapeDtypeStruct((B, FOX_HEADS, S, LANES), BF16),
                   jax.ShapeDtypeStruct((B, FOX_HEADS // 2, S // tk, LANES, tk), BF16)),
        grid=(B, S // ts),
        in_specs=[tile, _const_spec((1, D_MODEL)), _const_spec((D_MODEL, 3 * D_MODEL)),
                  _const_spec((D_MODEL, LANES)), _const_spec((1, LANES)),
                  _const_spec((1, D_MODEL)), _const_spec((1, D_MODEL))]
                 + [_const_spec(c.shape) for c in consts],
        out_specs=(pl.BlockSpec((None, FOX_HEADS, LANES, ts), lambda b, i: (b, 0, 0, i)),
                   pl.BlockSpec((None, FOX_HEADS, ts, LANES), lambda b, i: (b, 0, i, 0)),
                   pl.BlockSpec((None, FOX_HEADS // 2, ts // tk, LANES, tk), lambda b, i: (b, 0, i, 0, 0))),
        scratch_shapes=[pltpu.VMEM((1, LANES), F32)],
        compiler_params=_params("parallel", "arbitrary"),
        name="fox_proj",
    )(x, g, w_qkv, w_f, b_f, gq, gk, *consts)


def _fox_attn_kernel(qt_ref, qt_next_ref, k_ref, vt_ref, o_ref, st_ref, *, tq, tqs, tk):
    i = pl.program_id(2)
    nsub = tq // tqs
    nkb = tq // tk
    dh = FOX_HEAD_DIM
    chains = [(h, qs) for h in range(2) for qs in range(nsub)]

    def scores(n, jblk, q_ref=qt_ref):
        h, qs = chains[n]
        koff = pl.multiple_of(jblk * tk, tk)
        return _dot(k_ref[h, pl.ds(koff, tk), :], q_ref[h, :, qs * tqs:(qs + 1) * tqs])

    def softmax_step(st, m):
        m_new = jnp.maximum(m, jnp.max(st, axis=0, keepdims=True))
        return m_new, jnp.exp2(m - m_new), jnp.exp2(st - m_new).astype(BF16)

    ones_rows = jnp.ones((FOX_DEN_ROWS, tk), BF16)

    def values(n, jblk, alpha, p, a):
        h, _ = chains[n]
        vt = jnp.concatenate([vt_ref[jblk, h * dh:(h + 1) * dh, :], ones_rows], axis=0)
        return alpha * a + _dot(vt, p)

    nch = len(chains)

    def block(jblk, slot, states, masks, next_masks):
        cur = slot * nch
        nxt = nch - cur
        out = list(states)

        def issue_next(n):
            if n >= nch:
                return
            if next_masks is None:
                st_ref[nxt + n] = scores(n, 0, qt_next_ref)
            elif next_masks[n] is not False:
                st_ref[nxt + n] = scores(n, jblk + 1)

        issue_next(0)
        for n in range(nch):
            issue_next(n + 1)
            if masks[n] is False:
                continue
            st = st_ref[cur + n]
            if masks[n] is not None:
                st = jnp.where(masks[n], st, NEG_BIG)
            m_new, alpha, p = softmax_step(st, states[n][0])
            out[n] = (m_new, values(n, jblk, alpha, p, states[n][1]))
        return tuple(out)

    row = lax.broadcasted_iota(jnp.int32, (tk, tqs), 0)
    col = lax.broadcasted_iota(jnp.int32, (tk, tqs), 1)
    diag_masks = []
    for kb in range(nkb):
        masks = []
        for _, qs in chains:
            k_lo, q_lo = kb * tk, qs * tqs
            if k_lo > q_lo + tqs - 1:
                masks.append(False)
            else:
                masks.append(None if k_lo + tk - 1 <= q_lo else (row + k_lo <= col + q_lo))
        diag_masks.append(masks)
    no_mask = [None] * len(chains)

    states = tuple((jnp.full((1, tqs), NEG_BIG, F32), jnp.zeros((dh + FOX_DEN_ROWS, tqs), F32))
                   for _ in chains)

    @pl.when(i == 0)
    def _():
        for n in range(nch):
            st_ref[n] = scores(n, 0)

    def one_trip(jj, c):
        for u in range(FOX_BLOCKS_PER_TRIP):
            c = block(FOX_BLOCKS_PER_TRIP * jj + u, u % 2, c, no_mask, no_mask)
        return c

    states = lax.fori_loop(0, i * (nkb // FOX_BLOCKS_PER_TRIP), one_trip, states)
    for kb in range(nkb):
        states = block(i * nkb + kb, kb % 2, states, diag_masks[kb], diag_masks[kb + 1] if kb + 1 < nkb else None)

    def normalized(n):
        acc = states[n][1]
        return acc[:dh] / acc[dh:dh + 1]

    out_t = jnp.concatenate(
        [jnp.concatenate([normalized(h * nsub + qs) for qs in range(nsub)], axis=1) for h in range(2)],
        axis=0)
    o_ref[...] = out_t.T.astype(BF16)


def _fox_attn(qt, ka, vt):
    B, H, _, S = qt.shape
    tq, tqs, tk = min(FOX_TQ, S), min(FOX_TQS, S), min(FOX_TK, S)
    kern = functools.partial(_fox_attn_kernel, tq=tq, tqs=tqs, tk=tk)
    assert S % tq == 0 and tq % tqs == 0 and (tq // tk) % FOX_BLOCKS_PER_TRIP == 0
    last = S // tq - 1
    return pl.pallas_call(
        kern,
        out_shape=jax.ShapeDtypeStruct((B, S, D_MODEL), BF16),
        grid=(B, H // 2, S // tq),
        in_specs=[
            pl.BlockSpec((None, 2, LANES, tq), lambda b, p, i: (b, p, 0, i)),
            pl.BlockSpec((None, 2, LANES, tq), lambda b, p, i: (b, p, 0, jnp.minimum(i + 1, last))),
            pl.BlockSpec((None, 2, S, LANES), lambda b, p, i: (b, p, 0, 0)),
            pl.BlockSpec((None, None, S // tk, LANES, tk), lambda b, p, i: (b, p, 0, 0, 0)),
        ],
        out_specs=pl.BlockSpec((None, tq, LANES), lambda b, p, i: (b, i, p)),
        scratch_shapes=[pltpu.VMEM((2 * 2 * (tq // tqs), tk, tqs), F32)],
        compiler_params=_params("parallel", "parallel", "arbitrary"),
        name="fox_attn",
    )(qt, qt, ka, vt)


GLA_SCALE = GLA_DK_HEAD ** -0.5
GLA_TILE = 512


def _gla_kernel(x_ref, g_ref, win_ref, wg1_ref, wg2_ref, bg_ref, og_ref, y_ref, st_ref, *, ts, nb):
    @pl.when(pl.program_id(0) == 0)
    def _():
        st_ref[...] = jnp.zeros_like(st_ref)

    C = GLA_CHUNK
    hn = _rms(x_ref[...].reshape(nb * ts, D_MODEL), g_ref[...]).astype(BF16)
    tri = (lax.broadcasted_iota(jnp.int32, (C, C), 0)
           >= lax.broadcasted_iota(jnp.int32, (C, C), 1))
    tri3 = jnp.concatenate([tri.astype(BF16)] * N_SPLIT, axis=1)
    og = og_ref[...]
    dks = [slice(h * GLA_DK_HEAD, (h + 1) * GLA_DK_HEAD) for h in range(GLA_HEADS)]
    dvs = [slice(h * GLA_DV_HEAD, (h + 1) * GLA_DV_HEAD) for h in range(GLA_HEADS)]
    items = [(bi, h) for bi in range(nb) for h in range(GLA_HEADS)]
    proj = _dot(hn, win_ref[...])
    gate = _dot(_dot(hn, wg1_ref[...]).astype(BF16), wg2_ref[...]) + bg_ref[...]
    log_a = _log_sigmoid(gate) / GLA_TAU
    for c in range(ts // C):
        rows = [slice(bi * ts + c * C, bi * ts + (c + 1) * C) for bi in range(nb)]
        q_t, k_t, k_dec, decay = [], [], [], []
        for bi in range(nb):
            b = _dot(tri3, jnp.concatenate(_split3(log_a[rows[bi]]), axis=0))
            b_last = b[C - 1:C, :]
            kc = proj[rows[bi], GLA_DK:2 * GLA_DK]
            q_t.append(proj[rows[bi], :GLA_DK] * GLA_SCALE * jnp.exp(b))
            k_t.append(kc * jnp.exp(-b))
            k_dec.append(kc * jnp.exp(b_last - b))
            decay.append(jnp.exp(b_last))
        qs = [q_t[bi][:, dks[h]].astype(BF16) for bi, h in items]
        vs = [proj[rows[bi], 2 * GLA_DK + h * GLA_DV_HEAD:2 * GLA_DK + (h + 1) * GLA_DV_HEAD].astype(BF16)
              for bi, h in items]
        atts = [lax.dot_general(qs[n], k_t[bi][:, dks[h]].astype(BF16), _NT, preferred_element_type=F32)
                for n, (bi, h) in enumerate(items)]
        sts = [st_ref[bi, h] for bi, h in items]
        inter = [lax.dot_general(qs[n], sts[n].astype(BF16), _NT, preferred_element_type=F32)
                 for n in range(len(items))]
        for n, (bi, h) in enumerate(items):
            st_ref[bi, h] = sts[n] * decay[bi][:, dks[h]] + lax.dot_general(
                vs[n], k_dec[bi][:, dks[h]].astype(BF16), _TN, preferred_element_type=F32)
        for n, (bi, h) in enumerate(items):
            o = _dot(jnp.where(tri, atts[n], 0.0).astype(BF16), vs[n]) + inter[n]
            on = o * lax.rsqrt(jnp.mean(o * o, axis=-1, keepdims=True) + RMS_EPS) * og
            r = proj[rows[bi], 2 * GLA_DK + GLA_DV + h * GLA_DV_HEAD:2 * GLA_DK + GLA_DV + (h + 1) * GLA_DV_HEAD]
            y_ref[bi, c * C:(c + 1) * C, dvs[h]] = (on * (r * _sigmoid(r))).astype(BF16)


def _gla(x, g, w_in, w_g1, w_g2, b_g, o_g):
    B, S, _ = x.shape
    ts = min(GLA_TILE, S)
    kern = functools.partial(_gla_kernel, ts=ts, nb=B)
    tile = pl.BlockSpec((B, ts, D_MODEL), lambda i: (0, i, 0))
    return pl.pallas_call(
        kern,
        out_shape=jax.ShapeDtypeStruct((B, S, D_MODEL), BF16),
        grid=(S // ts,),
        in_specs=[tile, _const_spec((1, D_MODEL)), _const_spec((D_MODEL, 2 * GLA_DK + 2 * GLA_DV)),
                  _const_spec((D_MODEL, LANES)), _const_spec((LANES, GLA_DK)),
                  _const_spec((1, GLA_DK)), _const_spec((1, GLA_DV_HEAD))],
        out_specs=tile,
        scratch_shapes=[pltpu.VMEM((B, GLA_HEADS, GLA_DV_HEAD, GLA_DK_HEAD), F32)],
        compiler_params=_params("arbitrary"),
        name="gla",
    )(x, g, w_in, w_g1, w_g2, b_g, o_g)


def _row(v):
    return v.reshape(1, -1).astype(F32)


def _pad_cols(w, n):
    return jnp.pad(w, ((0, 0), (0, n - w.shape[1])))


def kernel(x, mix_norm, mlp_norm, mlp_w1, mlp_w2,
           a_w_in, a_b_in, a_conv_w, a_conv_b, a_ln_g, a_ln_b, a_w_out, a_b_out,
           b_w_in, b_conv_w, b_w_out,
           c_w_qkv, c_q_norm, c_k_norm, c_w_f, c_b_f, c_w_out,
           d_w_in, d_w_g1, d_w_g2, d_b_g, d_o_norm, d_w_out):
    B, S, _ = x.shape
    depth = mix_norm.shape[0]
    zero_bias = jnp.zeros((1, D_MODEL), F32)
    w1_all, w2_all = mlp_w1.astype(BF16), mlp_w2.astype(BF16)
    for i in range(depth):
        m, j = i % 4, i // 4
        g = _row(mix_norm[i])
        if m == 0:
            y = _conformer(x, g, a_w_in[j].astype(BF16), _row(a_b_in[j]), a_conv_w[j], _row(a_conv_b[j]),
                           _row(a_ln_g[j]), _row(a_ln_b[j]))
            wo, bo = a_w_out[j], _row(a_b_out[j])
        elif m == 1:
            y = _short_conv(x, g, b_w_in[j].astype(BF16), b_conv_w[j])
            wo, bo = b_w_out[j], zero_bias
        elif m == 2:
            gq = _row(jnp.tile(c_q_norm[j], FOX_HEADS)) * (FOX_SCALE * LOG2E)
            gk = _row(jnp.tile(c_k_norm[j], FOX_HEADS))
            qt, ka, vt = _fox_proj(x, g, c_w_qkv[j].astype(BF16), _pad_cols(c_w_f[j], LANES).astype(BF16),
                                  _pad_cols(_row(c_b_f[j]), LANES), gq, gk)
            y = _fox_attn(qt, ka, vt)
            wo, bo = c_w_out[j], zero_bias
        else:
            w_g2 = jnp.pad(d_w_g2[j], ((0, LANES - GLA_GATE_RANK), (0, 0)))
            y = _gla(x, g, d_w_in[j].astype(BF16), _pad_cols(d_w_g1[j], LANES).astype(BF16),
                     w_g2.astype(BF16), _row(d_b_g[j]), _row(d_o_norm[j]))
            wo, bo = d_w_out[j], zero_bias
        x = _proj_mlp(x.reshape(B * S, D_MODEL), y.reshape(B * S, D_MODEL), wo.astype(BF16), bo,
                      _row(mlp_norm[i]), w1_all, w2_all, i).reshape(B, S, D_MODEL)
    return x
```
